```python
import jax, jax.numpy as jnp
from jax import lax
import numpy as np

D_MODEL = 2048
BATCH = 4
SEQ = 2048
DEPTH = 1

D_MIX = D_MODEL
HEAD_DIM = 128
M_HEADS = 8
A_HEADS = 8
D_M = M_HEADS * HEAD_DIM
D_A = A_HEADS * HEAD_DIM
CONV_K = 4
MLSTM_CHUNK = 128
DILATED_PATTERNS = ((128, 1), (512, 4), (2048, 16))
REL_BUCKETS = 32
REL_MAX_DIST = 2048
N_EXPERTS = 256
TOP_K = 8
N_GROUPS = 8
TOPK_GROUPS = 4
D_EXPERT = 512
ROUTED_SCALE = 2.5
MOE_BLOCK = 64
NORM_EPS = 1e-6
D_PROJ = 4 * D_M + 3 * D_A + 2 * M_HEADS

kernel_name = "hymba_mlstm_dilated_attn_moe_layer"


def rms_norm(x, g):
    xf = x.astype(jnp.float32)
    y = xf * lax.rsqrt(jnp.mean(xf * xf, axis=-1, keepdims=True) + NORM_EPS)
    return (y * g.astype(jnp.float32)).astype(x.dtype)


def head_rms_norm(h, g):
    B, S, D = h.shape
    H = D // HEAD_DIM
    return rms_norm(h.reshape(B, S, H, HEAD_DIM), g.reshape(H, HEAD_DIM)).reshape(B, S, D)


def to_heads(a):
    B, S, D = a.shape
    return a.reshape(B, S, D // HEAD_DIM, HEAD_DIM).transpose(0, 2, 1, 3)


def from_heads(a):
    B, H, S, Dh = a.shape
    return a.transpose(0, 2, 1, 3).reshape(B, S, H * Dh)


def causal_depthwise_conv(x, w, b):
    S = x.shape[1]
    K = w.shape[0]
    xp = jnp.pad(x, ((0, 0), (K - 1, 0), (0, 0)))
    return sum(xp[:, j:j + S] * w[j] for j in range(K)) + b


def mlstm_chunkwise(q, k, v, ig, logf):
    B, H, S, Dh = q.shape
    Lc = MLSTM_CHUNK
    NC = S // Lc

    def chunks(a):
        return jnp.moveaxis(a.reshape(a.shape[:2] + (NC, Lc) + a.shape[3:]), 2, 0)

    qc, kc, vc, ic = chunks(q), chunks(k), chunks(v), chunks(ig)
    bc = jnp.cumsum(chunks(logf), axis=-1)
    causal = jnp.tril(jnp.ones((Lc, Lc), dtype=bool))

    def step(carry, inp):
        C, n, m = carry
        qt, kt, vt, it, bt = inp
        dmat = jnp.where(causal, bt[..., :, None] - bt[..., None, :] + it[..., None, :], -jnp.inf)
        inter = bt + m[..., None]
        m_t = jnp.maximum(inter, dmat.max(-1))
        w_intra = jnp.exp(dmat - m_t[..., None])
        w_inter = jnp.exp(inter - m_t)
        s = jnp.einsum('bhtd,bhsd->bhts', qt, kt) * w_intra
        num = jnp.einsum('bhts,bhsd->bhtd', s, vt) + w_inter[..., None] * jnp.einsum('bhvk,bhtk->bhtv', C, qt)
        den = s.sum(-1) + w_inter * jnp.einsum('bhk,bhtk->bht', n, qt)
        h = num / jnp.maximum(jnp.abs(den), jnp.exp(-m_t))[..., None]
        b_last = bt[..., -1]
        log_w = b_last[..., None] - bt + it
        m_new = jnp.maximum(b_last + m, log_w.max(-1))
        w_upd = jnp.exp(log_w - m_new[..., None])
        decay = jnp.exp(b_last + m - m_new)
        C = decay[..., None, None] * C + jnp.einsum('bhsv,bhsk->bhvk', vt * w_upd[..., None], kt)
        n = decay[..., None] * n + jnp.einsum('bhs,bhsk->bhk', w_upd, kt)
        return (C, n, m_new), h

    init = (jnp.zeros((B, H, Dh, Dh), jnp.float32), jnp.zeros((B, H, Dh), jnp.float32),
            jnp.zeros((B, H), jnp.float32))
    _, hc = lax.scan(step, init, (qc, kc, vc, ic, bc))
    return jnp.moveaxis(hc, 0, 2).reshape(B, H, S, Dh)


def t5_causal_bucket(dist):
    max_exact = REL_BUCKETS // 2
    d = np.maximum(dist, 1).astype(np.float32)
    large = max_exact + (np.log(d / max_exact) / np.log(REL_MAX_DIST / max_exact)
                         * (REL_BUCKETS - max_exact)).astype(np.int32)
    return np.where(dist < max_exact, dist, np.minimum(large, REL_BUCKETS - 1)).astype(np.int32)


def dilated_branch(q, k, v, rel_bias, window, dilation):
    B, H, S, Dh = q.shape
    W = window // dilation
    L = S // dilation
    nb = -(-L // W)
    Lp = nb * W

    def fold(a):
        a = a.reshape(B, H, L, dilation, Dh).transpose(0, 1, 3, 2, 4)
        return jnp.pad(a, ((0, 0), (0, 0), (0, 0), (0, Lp - L), (0, 0)))

    qf, kf, vf = fold(q), fold(k), fold(v)
    qb = qf.reshape(B, H, dilation, nb, W, Dh)

    def band(a):
        ap = jnp.pad(a, ((0, 0), (0, 0), (0, 0), (W, 0), (0, 0)))
        prev = ap[:, :, :, :Lp].reshape(B, H, dilation, nb, W, Dh)
        cur = a.reshape(B, H, dilation, nb, W, Dh)
        return jnp.concatenate([prev, cur], axis=4)

    kb, vb = band(kf), band(vf)
    a_idx = np.arange(W)[:, None]
    c_idx = np.arange(2 * W)[None, :]
    j = W + a_idx - c_idx
    local_ok = (j >= 0) & (j <= W)
    key_ok = (np.arange(nb)[:, None, None] * W + c_idx[None] - W) >= 0
    mask = local_ok[None] & key_ok
    bias = rel_bias[t5_causal_bucket(np.clip(j, 0, W) * dilation)]
    bias = jnp.transpose(bias, (2, 0, 1)).astype(jnp.float32)
    s = (jnp.einsum('bhrnad,bhrncd->bhrnac', qb, kb).astype(jnp.float32) * HEAD_DIM ** -0.5
         + bias[None, :, None, None])
    s = jnp.where(mask, s, -jnp.inf)
    m = s.max(-1, keepdims=True)
    p = jnp.exp(s - m)
    den = p.sum(-1)
    o = jnp.einsum('bhrnac,bhrncd->bhrnad', p, vb.astype(jnp.float32)) / den[..., None]
    lse = m[..., 0] + jnp.log(den)
    o = o.reshape(B, H, dilation, Lp, Dh)[:, :, :, :L].transpose(0, 1, 3, 2, 4).reshape(B, H, S, Dh)
    lse = lse.reshape(B, H, dilation, Lp)[..., :L].transpose(0, 1, 3, 2).reshape(B, H, S)
    return o, lse


def dilated_attention(q, k, v, rel_bias):
    outs, lses = [], []
    for window, dilation in DILATED_PATTERNS:
        o, lse = dilated_branch(q, k, v, rel_bias, window, dilation)
        outs.append(o)
        lses.append(lse)
    wts = jax.nn.softmax(jnp.stack(lses), axis=0)
    return jnp.einsum('pbhs,pbhsd->bhsd', wts, jnp.stack(outs))


def hybrid_mixer(h, w_in, b_if, conv_w, conv_b, head_gains, rel_bias, w_out):
    f32 = jnp.float32
    proj = h @ w_in
    q_m, k_m, v_m, o_m, q_a, k_a, v_a, gates = jnp.split(
        proj, np.cumsum([D_M] * 4 + [D_A] * 3), axis=-1)
    qk_m = jax.nn.silu(causal_depthwise_conv(jnp.concatenate([q_m, k_m], axis=-1), conv_w, conv_b))
    q_m, k_m = jnp.split(qk_m, 2, axis=-1)
    gates = (gates + b_if).astype(f32)
    ig = gates[..., :M_HEADS].transpose(0, 2, 1)
    logf = jax.nn.log_sigmoid(gates[..., M_HEADS:]).transpose(0, 2, 1)
    hm = mlstm_chunkwise(to_heads(q_m).astype(f32), to_heads(k_m).astype(f32) * HEAD_DIM ** -0.5,
                         to_heads(v_m).astype(f32), ig, logf)
    hm = head_rms_norm(from_heads(hm).astype(h.dtype), head_gains[:D_M]) * jax.nn.sigmoid(o_m)
    ha = dilated_attention(to_heads(q_a), to_heads(k_a), to_heads(v_a), rel_bias)
    ha = head_rms_norm(from_heads(ha).astype(h.dtype), head_gains[D_M:])
    return jnp.concatenate([hm, ha], axis=-1) @ w_out


def swiglu(x, wg, wu, wd):
    return (jax.nn.silu(x @ wg) * (x @ wu)) @ wd


def moe_ffn(h, w_router, router_bias, w_gate, w_up, w_down, ws_gate, ws_up, ws_down):
    T, D = h.shape
    scores = jax.nn.sigmoid((h @ w_router).astype(jnp.float32))
    sel = scores + router_bias.astype(jnp.float32)
    gscore = lax.top_k(sel.reshape(T, N_GROUPS, N_EXPERTS // N_GROUPS), 2)[0].sum(-1)
    _, gidx = lax.top_k(gscore, TOPK_GROUPS)
    gmask = jnp.zeros((T, N_GROUPS), dtype=bool).at[jnp.arange(T)[:, None], gidx].set(True)
    emask = jnp.repeat(gmask, N_EXPERTS // N_GROUPS, axis=1)
    _, eidx = lax.top_k(jnp.where(emask, sel, -jnp.inf), TOP_K)
    g = jnp.take_along_axis(scores, eidx, axis=1)
    g = g / g.sum(-1, keepdims=True) * ROUTED_SCALE
    A = T * TOP_K
    NB = (A + N_EXPERTS * (MOE_BLOCK - 1) + MOE_BLOCK - 1) // MOE_BLOCK
    P = NB * MOE_BLOCK
    e_flat = eidx.reshape(-1)
    t_flat = jnp.repeat(jnp.arange(T, dtype=jnp.int32), TOP_K)
    g_flat = g.reshape(-1)
    order = jnp.argsort(e_flat)
    e_s, t_s, g_s = e_flat[order], t_flat[order], g_flat[order]
    counts = jnp.bincount(e_flat, length=N_EXPERTS)
    starts = jnp.cumsum(counts) - counts
    padded = (counts + MOE_BLOCK - 1) // MOE_BLOCK * MOE_BLOCK
    pends = jnp.cumsum(padded)
    pstarts = pends - padded
    dest = pstarts[e_s] + (jnp.arange(A) - starts[e_s])
    buf_tok = jnp.full((P,), T, dtype=jnp.int32).at[dest].set(t_s)
    buf_w = jnp.zeros((P,), jnp.float32).at[dest].set(g_s)
    block_expert = jnp.clip(jnp.searchsorted(pends, jnp.arange(NB) * MOE_BLOCK, side='right'),
                            0, N_EXPERTS - 1)
    h_pad = jnp.concatenate([h, jnp.zeros((1, D), h.dtype)], axis=0)

    def expert_block(args):
        tok, wt, e = args
        y = swiglu(h_pad[tok], w_gate[e], w_up[e], w_down[e])
        return y * wt[:, None].astype(y.dtype)

    ys = lax.map(expert_block, (buf_tok.reshape(NB, MOE_BLOCK), buf_w.reshape(NB, MOE_BLOCK), block_expert))
    routed = jnp.zeros((T + 1, D), ys.dtype).at[buf_tok].add(ys.reshape(P, D))[:T]
    return swiglu(h, ws_gate, ws_up, ws_down) + routed.astype(h.dtype)


def setup_inputs(seed: int = 0) -> dict:
    key = jax.random.key(seed)
    ks = jax.random.split(key, 21)
    L = DEPTH

    def nrm(k, shape, scale):
        return jax.random.normal(k, shape, jnp.float32) * scale

    x = nrm(ks[0], (BATCH, SEQ, D_MODEL), 1.0)
    c = nrm(ks[1], (BATCH, D_MODEL), 1.0)
    w_ada = nrm(ks[2], (L, D_MODEL, 6 * D_MODEL), 0.5 * D_MODEL ** -0.5)
    b_ada = nrm(ks[3], (L, 6 * D_MODEL), 0.02)
    norm_gains = 1.0 + nrm(ks[4], (L, 4, D_MODEL), 0.05)
    w_in = nrm(ks[5], (L, D_MODEL, D_PROJ), D_MODEL ** -0.5)
    f_bias = jnp.linspace(3.0, 6.0, M_HEADS, dtype=jnp.float32)
    b_if = jnp.concatenate([nrm(ks[6], (L, M_HEADS), 0.1),
                            f_bias[None] + nrm(ks[7], (L, M_HEADS), 0.1)], axis=-1)
    conv_w = nrm(ks[8], (L, CONV_K, 2 * D_M), CONV_K ** -0.5)
    conv_b = nrm(ks[9], (L, 2 * D_M), 0.02)
    head_gains = 1.0 + nrm(ks[10], (L, D_M + D_A), 0.05)
    rel_bias = nrm(ks[11], (REL_BUCKETS, A_HEADS), 0.5)
    w_out = nrm(ks[12], (L, D_MIX, D_MODEL), D_MIX ** -0.5)
    w_router = nrm(ks[13], (L, D_MODEL, N_EXPERTS), D_MODEL ** -0.5)
    router_bias = nrm(ks[14], (L, N_EXPERTS), 0.01)
    w_exp_gate = nrm(ks[15], (L, N_EXPERTS, D_MODEL, D_EXPERT), D_MODEL ** -0.5)
    w_exp_up = nrm(ks[16], (L, N_EXPERTS, D_MODEL, D_EXPERT), D_MODEL ** -0.5)
    w_exp_down = nrm(ks[17], (L, N_EXPERTS, D_EXPERT, D_MODEL), D_EXPERT ** -0.5)
    w_sh_gate = nrm(ks[18], (L, D_MODEL, D_EXPERT), D_MODEL ** -0.5)
    w_sh_up = nrm(ks[19], (L, D_MODEL, D_EXPERT), D_MODEL ** -0.5)
    w_sh_down = nrm(ks[20], (L, D_EXPERT, D_MODEL), D_EXPERT ** -0.5)
    return {"x": x, "c": c, "w_ada": w_ada, "b_ada": b_ada, "norm_gains": norm_gains, "w_in": w_in,
            "b_if": b_if, "conv_w": conv_w, "conv_b": conv_b, "head_gains": head_gains,
            "rel_bias": rel_bias, "w_out": w_out, "w_router": w_router, "router_bias": router_bias,
            "w_exp_gate": w_exp_gate, "w_exp_up": w_exp_up, "w_exp_down": w_exp_down,
            "w_sh_gate": w_sh_gate, "w_sh_up": w_sh_up, "w_sh_down": w_sh_down}


def reference(x, c, w_ada, b_ada, norm_gains, w_in, b_if, conv_w, conv_b, head_gains, rel_bias, w_out,
              w_router, router_bias, w_exp_gate, w_exp_up, w_exp_down, w_sh_gate, w_sh_up, w_sh_down):
    B, S, D = x.shape
    for l in range(DEPTH):
        mod = jax.nn.silu(c) @ w_ada[l] + b_ada[l]
        sh_a, sc_a, gt_a, sh_f, sc_f, gt_f = [m[:, None, :] for m in jnp.split(mod, 6, axis=-1)]
        g = norm_gains[l]
        h = rms_norm(x, g[0]) * (1 + sc_a) + sh_a
        y = hybrid_mixer(h, w_in[l], b_if[l], conv_w[l], conv_b[l], head_gains[l], rel_bias, w_out[l])
        x = x + gt_a * rms_norm(y, g[1])
        h = rms_norm(x, g[2]) * (1 + sc_f) + sh_f
        y = moe_ffn(h.reshape(B * S, D), w_router[l], router_bias[l], w_exp_gate[l], w_exp_up[l],
                    w_exp_down[l], w_sh_gate[l], w_sh_up[l], w_sh_down[l]).reshape(B, S, D)
        x = x + gt_f * rms_norm(y, g[3])
    return x
```

```python
import functools

import numpy as np
import jax
import jax.numpy as jnp
from jax import lax
from jax.experimental import pallas as pl
from jax.experimental.pallas import tpu as pltpu

D_MODEL = 2048
BATCH = 4
SEQ = 2048
HEAD_DIM = 128
M_HEADS = 8
A_HEADS = 8
D_M = M_HEADS * HEAD_DIM
D_A = A_HEADS * HEAD_DIM
CONV_K = 4
MLSTM_CHUNK = 128
DILATED_PATTERNS = ((128, 1), (512, 4), (2048, 16))
REL_BUCKETS = 32
REL_MAX_DIST = 2048
N_EXPERTS = 256
TOP_K = 8
N_GROUPS = 8
TOPK_GROUPS = 4
D_EXPERT = 512
ROUTED_SCALE = 2.5
NORM_EPS = 1e-6
D_PROJ_MAIN = 4 * D_M + 3 * D_A
N_GATES = 2 * M_HEADS
LANES = 128
ATT_W = 128
MOE_TILE = 128
VMEM_LIMIT = 56 * 1024 * 1024

F32 = jnp.float32
BF16 = jnp.bfloat16
NEG_INF = float("-inf")


def _cparams(sem):
    return pltpu.CompilerParams(dimension_semantics=sem, vmem_limit_bytes=VMEM_LIMIT)


def _dot(a, b):
    return jnp.dot(a, b, preferred_element_type=F32)


def _dot_nt(a, b):
    return lax.dot_general(a, b, (((1,), (1,)), ((), ())), preferred_element_type=F32)


def _dot_tn(a, b):
    return lax.dot_general(a, b, (((0,), (0,)), ((), ())), preferred_element_type=F32)


def _rms(x, g):
    ms = jnp.mean(x * x, axis=-1, keepdims=True)
    return x * lax.rsqrt(ms + NORM_EPS) * g


def _silu(x):
    return x * jax.nn.sigmoid(x)


def _ada_kernel(c_ref, w_ref, b_ref, o_ref):
    a = _silu(c_ref[...]).astype(BF16)
    o_ref[...] = _dot(a, w_ref[...].astype(BF16)) + b_ref[...]


def _ada(c_pad, w_ada, b_ada):
    tn = 1024
    n = w_ada.shape[1]
    return pl.pallas_call(
        _ada_kernel,
        grid=(n // tn,),
        in_specs=[pl.BlockSpec((8, D_MODEL), lambda j: (0, 0)),
                  pl.BlockSpec((D_MODEL, tn), lambda j: (0, j)),
                  pl.BlockSpec((1, tn), lambda j: (0, j))],
        out_specs=pl.BlockSpec((8, tn), lambda j: (0, j)),
        out_shape=jax.ShapeDtypeStruct((8, n), F32),
        compiler_params=_cparams(("arbitrary",)),
        name="ada_mod",
    )(c_pad, w_ada, b_ada)


def _inproj_kernel(x_ref, mod_ref, g_ref, w_ref, wg_ref, bif_ref, proj_ref, gates_ref, h_sc):
    @pl.when(pl.program_id(1) == 0)
    def _():
        h = _rms(x_ref[...], g_ref[...]) * (1.0 + mod_ref[1:2, :]) + mod_ref[0:1, :]
        hb = h.astype(BF16)
        h_sc[...] = hb
        g = _dot(hb, wg_ref[...].astype(BF16)) + bif_ref[...]
        lane = lax.broadcasted_iota(jnp.int32, g.shape, 1)
        logf = jnp.minimum(g, 0.0) - jnp.log1p(jnp.exp(-jnp.abs(g)))
        gates_ref[...] = jnp.where(lane < M_HEADS, g, logf)

    proj_ref[...] = _dot(h_sc[...], w_ref[...].astype(BF16))


def _inproj(x2, mod3, g0, w_in, w_gates, bif):
    tm, tn = 1024, 512
    t = x2.shape[0]
    return pl.pallas_call(
        _inproj_kernel,
        grid=(t // tm, D_PROJ_MAIN // tn),
        in_specs=[pl.BlockSpec((tm, D_MODEL), lambda i, j: (i, 0)),
                  pl.BlockSpec((None, 6, D_MODEL), lambda i, j: (i // (SEQ // tm), 0, 0)),
                  pl.BlockSpec((1, D_MODEL), lambda i, j: (0, 0)),
                  pl.BlockSpec((D_MODEL, tn), lambda i, j: (0, j)),
                  pl.BlockSpec((D_MODEL, LANES), lambda i, j: (0, 0)),
                  pl.BlockSpec((1, LANES), lambda i, j: (0, 0))],
        out_specs=[pl.BlockSpec((tm, tn), lambda i, j: (i, j)),
                   pl.BlockSpec((tm, LANES), lambda i, j: (i, 0))],
        out_shape=[jax.ShapeDtypeStruct((t, D_PROJ_MAIN), F32),
                   jax.ShapeDtypeStruct((t, LANES), F32)],
        scratch_shapes=[pltpu.VMEM((tm, D_MODEL), BF16)],
        compiler_params=_cparams(("arbitrary", "arbitrary")),
        name="in_proj",
    )(x2, mod3, g0, w_in, w_gates, bif)


def _mlstm_kernel(q_ref, k_ref, v_ref, g_ref, gt_ref, cwq_ref, cwk_ref, cbq_ref, cbk_ref, o_ref,
                  ct_sc, n_sc, m_sc, pq_sc, pk_sc):
    lc = MLSTM_CHUNK

    @pl.when(pl.program_id(1) == 0)
    def _():
        ct_sc[...] = jnp.zeros_like(ct_sc)
        n_sc[...] = jnp.zeros_like(n_sc)
        m_sc[...] = jnp.zeros_like(m_sc)
        pq_sc[...] = jnp.zeros_like(pq_sc)
        pk_sc[...] = jnp.zeros_like(pk_sc)

    row_w = lax.broadcasted_iota(jnp.int32, (lc, D_M), 0)

    def conv(cur, prev, w_ref, b_ref):
        acc = cur * w_ref[CONV_K - 1:CONV_K, :] + b_ref[...]
        for s in range(1, CONV_K):
            sh = jnp.where(row_w >= s, pltpu.roll(cur, s, 0), pltpu.roll(prev, s, 0))
            acc = acc + sh * w_ref[CONV_K - 1 - s:CONV_K - s, :]
        return _silu(acc)

    q_raw = q_ref[...]
    k_raw = k_ref[...]
    q_all = conv(q_raw, pq_sc[...], cwq_ref, cbq_ref)
    k_all = conv(k_raw, pk_sc[...], cwk_ref, cbk_ref) * (HEAD_DIM ** -0.5)
    pq_sc[...] = q_raw
    pk_sc[...] = k_raw

    row = lax.broadcasted_iota(jnp.int32, (lc, lc), 0)
    col = lax.broadcasted_iota(jnp.int32, (lc, lc), 1)
    causal = col <= row
    tril = causal.astype(F32)
    triu = (row <= col).astype(F32)
    g = g_ref[...]
    gt = gt_ref[...]
    b_cols = jnp.dot(tril, g, precision=lax.Precision.HIGHEST, preferred_element_type=F32)
    b_rows = jnp.dot(gt, triu, precision=lax.Precision.HIGHEST, preferred_element_type=F32)

    for h in range(M_HEADS):
        sl = slice(h * HEAD_DIM, (h + 1) * HEAD_DIM)
        i_col = g[:, h:h + 1]
        b_col = b_cols[:, M_HEADS + h:M_HEADS + h + 1]
        i_row = gt[h:h + 1, :]
        b_row = b_rows[M_HEADS + h:M_HEADS + h + 1, :]
        m_prev = m_sc[h:h + 1, 0:1]
        n_prev = n_sc[h:h + 1, :]
        ct_prev = ct_sc[h]
        qh = q_all[:, sl]
        kh = k_all[:, sl]
        vh = v_ref[:, sl]
        qb = qh.astype(BF16)
        kb = kh.astype(BF16)

        dmat = jnp.where(causal, b_col - b_row + i_row, NEG_INF)
        inter = b_col + m_prev
        m_t = jnp.maximum(inter, jnp.max(dmat, axis=-1, keepdims=True))
        w_intra = jnp.exp(dmat - m_t)
        w_inter = jnp.exp(inter - m_t)
        s = _dot_nt(qb, kb) * w_intra
        num = _dot(s.astype(BF16), vh.astype(BF16)) + w_inter * _dot(qb, ct_prev.astype(BF16))
        den = jnp.sum(s, axis=-1, keepdims=True) + w_inter * jnp.sum(qh * n_prev, axis=-1, keepdims=True)
        o_ref[:, sl] = num / jnp.maximum(jnp.abs(den), jnp.exp(-m_t))

        b_last = b_col[lc - 1:lc, :]
        log_w = b_last - b_col + i_col
        m_new = jnp.maximum(b_last + m_prev, jnp.max(log_w, axis=0, keepdims=True))
        w_upd = jnp.exp(log_w - m_new)
        decay = jnp.exp(b_last + m_prev - m_new)
        ct_sc[h] = decay * ct_prev + _dot_tn(kb, (vh * w_upd).astype(BF16))
        n_sc[h:h + 1, :] = decay * n_prev + jnp.sum(kh * w_upd, axis=0, keepdims=True)
        m_sc[h:h + 1, :] = jnp.broadcast_to(m_new, (1, LANES))


def _mlstm(proj, gates, gates_t, conv_w, conv_b):
    lc = MLSTM_CHUNK
    nc = SEQ // lc
    t = proj.shape[0]
    return pl.pallas_call(
        _mlstm_kernel,
        grid=(BATCH, nc),
        in_specs=[pl.BlockSpec((lc, D_M), lambda b, c: (b * nc + c, 0)),
                  pl.BlockSpec((lc, D_M), lambda b, c: (b * nc + c, 1)),
                  pl.BlockSpec((lc, D_M), lambda b, c: (b * nc + c, 2)),
                  pl.BlockSpec((lc, LANES), lambda b, c: (b * nc + c, 0)),
                  pl.BlockSpec((None, N_GATES, lc), lambda b, c: (b, 0, c)),
                  pl.BlockSpec((CONV_K, D_M), lambda b, c: (0, 0)),
                  pl.BlockSpec((CONV_K, D_M), lambda b, c: (0, 1)),
                  pl.BlockSpec((1, D_M), lambda b, c: (0, 0)),
                  pl.BlockSpec((1, D_M), lambda b, c: (0, 1))],
        out_specs=pl.BlockSpec((lc, D_M), lambda b, c: (b * nc + c, 0)),
        out_shape=jax.ShapeDtypeStruct((t, D_M), F32),
        scratch_shapes=[pltpu.VMEM((M_HEADS, HEAD_DIM, HEAD_DIM), F32),
                        pltpu.VMEM((M_HEADS, HEAD_DIM), F32),
                        pltpu.VMEM((M_HEADS, LANES), F32),
                        pltpu.VMEM((lc, D_M), F32),
                        pltpu.VMEM((lc, D_M), F32)],
        compiler_params=_cparams(("arbitrary", "arbitrary")),
        name="mlstm",
    )(proj, proj, proj, gates, gates_t, conv_w, conv_w, conv_b, conv_b)


def _attn_kernel(*refs, has_prev):
    if has_prev:
        q_ref, kc_ref, vc_ref, kp_ref, vp_ref, bias_ref, o_ref, l_ref = refs
    else:
        q_ref, kc_ref, vc_ref, bias_ref, o_ref, l_ref = refs
    w = ATT_W
    row = lax.broadcasted_iota(jnp.int32, (w, w), 0)
    col = lax.broadcasted_iota(jnp.int32, (w, w), 1)
    cur_ok = col <= row
    scale = HEAD_DIM ** -0.5
    if has_prev:
        prev_ok = (col >= row) & (pl.program_id(2) > 0)
    for h in range(A_HEADS):
        sl = slice(h * HEAD_DIM, (h + 1) * HEAD_DIM)
        qb = q_ref[:, sl].astype(BF16)
        s_c = _dot_nt(qb, kc_ref[:, sl].astype(BF16)) * scale + bias_ref[h, :, w:2 * w]
        s_c = jnp.where(cur_ok, s_c, NEG_INF)
        m = jnp.max(s_c, axis=-1, keepdims=True)
        if has_prev:
            s_p = _dot_nt(qb, kp_ref[:, sl].astype(BF16)) * scale + bias_ref[h, :, 0:w]
            s_p = jnp.where(prev_ok, s_p, NEG_INF)
            m = jnp.maximum(m, jnp.max(s_p, axis=-1, keepdims=True))
        p_c = jnp.exp(s_c - m)
        den = jnp.sum(p_c, axis=-1, keepdims=True)
        acc = _dot(p_c.astype(BF16), vc_ref[:, sl].astype(BF16))
        if has_prev:
            p_p = jnp.exp(s_p - m)
            den = den + jnp.sum(p_p, axis=-1, keepdims=True)
            acc = acc + _dot(p_p.astype(BF16), vp_ref[:, sl].astype(BF16))
        o_ref[:, sl] = acc / den
        l_ref[:, sl] = jnp.broadcast_to(m + jnp.log(den), (w, HEAD_DIM))


def _attn_branch(proj, bias, dilation):
    w = ATT_W
    l = SEQ // dilation
    nb = l // w
    has_prev = nb > 1
    pv = proj.reshape(BATCH, l, dilation * D_PROJ_MAIN)
    nblk = D_PROJ_MAIN // D_A
    qi, ki, vi = nblk - 3, nblk - 2, nblk - 1

    def cur(blk):
        return pl.BlockSpec((None, w, D_A), lambda b, r, n: (b, n, r * nblk + blk))

    def prev(blk):
        return pl.BlockSpec((None, w, D_A), lambda b, r, n: (b, jnp.maximum(n - 1, 0), r * nblk + blk))

    in_specs = [cur(qi), cur(ki), cur(vi)]
    args = [pv, pv, pv]
    if has_prev:
        in_specs += [prev(ki), prev(vi)]
        args += [pv, pv]
    in_specs.append(pl.BlockSpec((A_HEADS, w, 2 * w), lambda b, r, n: (0, 0, 0)))
    args.append(bias)
    out_spec = pl.BlockSpec((None, w, D_A), lambda b, r, n: (b, n, r))
    o, lse = pl.pallas_call(
        functools.partial(_attn_kernel, has_prev=has_prev),
        grid=(BATCH, dilation, nb),
        in_specs=in_specs,
        out_specs=[out_spec, out_spec],
        out_shape=[jax.ShapeDtypeStruct((BATCH, l, dilation * D_A), F32)] * 2,
        compiler_params=_cparams(("arbitrary", "arbitrary", "arbitrary")),
        name=f"dilated_attn_d{dilation}",
    )(*args)
    return o.reshape(BATCH * SEQ, D_A), lse.reshape(BATCH * SEQ, D_A)


def _t5_causal_bucket(dist):
    max_exact = REL_BUCKETS // 2
    d = np.maximum(dist, 1).astype(np.float32)
    large = max_exact + (np.log(d / max_exact) / np.log(REL_MAX_DIST / max_exact)
                         * (REL_BUCKETS - max_exact)).astype(np.int32)
    return np.where(dist < max_exact, dist, np.minimum(large, REL_BUCKETS - 1)).astype(np.int32)


def _attn_bias(rel_bias, dilation):
    w = ATT_W
    j = w + np.arange(w)[:, None] - np.arange(2 * w)[None, :]
    bias = rel_bias[_t5_causal_bucket(np.clip(j, 0, w) * dilation)]
    return jnp.transpose(bias, (2, 0, 1)).astype(F32)


def _head_rms(x, g_ref, base):
    outs = []
    for h in range(x.shape[1] // HEAD_DIM):
        sl = slice(h * HEAD_DIM, (h + 1) * HEAD_DIM)
        outs.append(_rms(x[:, sl], g_ref[:, base + h * HEAD_DIM:base + (h + 1) * HEAD_DIM]))
    return outs


def _outproj_kernel(hm_ref, om_ref, o1_ref, o2_ref, o3_ref, l1_ref, l2_ref, l3_ref, x_ref, mod_ref,
                    hg_ref, g_ref, w_ref, out_ref, cat_sc):
    l1, l2, l3 = l1_ref[...], l2_ref[...], l3_ref[...]
    lm = jnp.maximum(jnp.maximum(l1, l2), l3)
    e1, e2, e3 = jnp.exp(l1 - lm), jnp.exp(l2 - lm), jnp.exp(l3 - lm)
    tot = e1 + e2 + e3
    ha = (e1 / tot) * o1_ref[...] + (e2 / tot) * o2_ref[...] + (e3 / tot) * o3_ref[...]
    gate = jax.nn.sigmoid(om_ref[...])
    for h, seg in enumerate(_head_rms(hm_ref[...], hg_ref, 0)):
        sl = slice(h * HEAD_DIM, (h + 1) * HEAD_DIM)
        cat_sc[:, sl] = (seg * gate[:, sl]).astype(BF16)
    for h, seg in enumerate(_head_rms(ha, hg_ref, D_M)):
        cat_sc[:, D_M + h * HEAD_DIM:D_M + (h + 1) * HEAD_DIM] = seg.astype(BF16)
    y = _dot(cat_sc[...], w_ref[...])
    out_ref[...] = x_ref[...] + mod_ref[2:3, :] * _rms(y, g_ref[...])


def _outproj(hm, proj, o1, o2, o3, l1, l2, l3, x2, mod3, head_gains, g1, w_out_bf):
    tm = 256
    t = x2.shape[0]
    half = lambda blk: pl.BlockSpec((tm, D_M), lambda i: (i, blk))
    full = pl.BlockSpec((tm, D_MODEL), lambda i: (i, 0))
    vec = pl.BlockSpec((1, D_MODEL), lambda i: (0, 0))
    return pl.pallas_call(
        _outproj_kernel,
        grid=(t // tm,),
        in_specs=[half(0), half(3)] + [half(0)] * 6 + [
            full,
            pl.BlockSpec((None, 6, D_MODEL), lambda i: (i // (SEQ // tm), 0, 0)),
            vec, vec,
            pl.BlockSpec((D_MODEL, D_MODEL), lambda i: (0, 0))],
        out_specs=full,
        out_shape=jax.ShapeDtypeStruct((t, D_MODEL), F32),
        scratch_shapes=[pltpu.VMEM((tm, D_MODEL), BF16)],
        compiler_params=_cparams(("arbitrary",)),
        name="out_proj",
    )(hm, proj, o1, o2, o3, l1, l2, l3, x2, mod3, head_gains, g1, w_out_bf)


def _router_kernel(x_ref, mod_ref, g_ref, w_ref, h_ref, s_ref):
    h = _rms(x_ref[...], g_ref[...]) * (1.0 + mod_ref[4:5, :]) + mod_ref[3:4, :]
    hb = h.astype(BF16)
    h_ref[...] = hb
    s_ref[...] = jax.nn.sigmoid(_dot(hb, w_ref[...].astype(BF16)))


def _router(x1, mod3, g2, w_router):
    tm = 512
    t = x1.shape[0]
    return pl.pallas_call(
        _router_kernel,
        grid=(t // tm,),
        in_specs=[pl.BlockSpec((tm, D_MODEL), lambda i: (i, 0)),
                  pl.BlockSpec((None, 6, D_MODEL), lambda i: (i // (SEQ // tm), 0, 0)),
                  pl.BlockSpec((1, D_MODEL), lambda i: (0, 0)),
                  pl.BlockSpec((D_MODEL, N_EXPERTS), lambda i: (0, 0))],
        out_specs=[pl.BlockSpec((tm, D_MODEL), lambda i: (i, 0)),
                   pl.BlockSpec((tm, N_EXPERTS), lambda i: (i, 0))],
        out_shape=[jax.ShapeDtypeStruct((t, D_MODEL), BF16),
                   jax.ShapeDtypeStruct((t, N_EXPERTS), F32)],
        compiler_params=_cparams(("arbitrary",)),
        name="ffn_norm_router",
    )(x1, mod3, g2, w_router)


def _experts_kernel(te_ref, x_ref, wt_ref, wg_ref, wu_ref, wd_ref, y_ref):
    del te_ref
    x = x_ref[...]
    a = _dot(x, wg_ref[...].astype(BF16))
    u = _dot(x, wu_ref[...].astype(BF16))
    hid = (_silu(a) * u).astype(BF16)
    y_ref[...] = _dot(hid, wd_ref[...].astype(BF16)) * wt_ref[...]


def _experts(tile_expert, xs, wts, w_gate, w_up, w_down):
    tm = MOE_TILE
    p = xs.shape[0]
    grid_spec = pltpu.PrefetchScalarGridSpec(
        num_scalar_prefetch=1,
        grid=(p // tm,),
        in_specs=[pl.BlockSpec((tm, D_MODEL), lambda i, te: (i, 0)),
                  pl.BlockSpec((tm, 1), lambda i, te: (i, 0)),
                  pl.BlockSpec((None, D_MODEL, D_EXPERT), lambda i, te: (te[i], 0, 0)),
                  pl.BlockSpec((None, D_MODEL, D_EXPERT), lambda i, te: (te[i], 0, 0)),
                  pl.BlockSpec((None, D_EXPERT, D_MODEL), lambda i, te: (te[i], 0, 0))],
        out_specs=pl.BlockSpec((tm, D_MODEL), lambda i, te: (i, 0)),
    )
    return pl.pallas_call(
        _experts_kernel,
        grid_spec=grid_spec,
        out_shape=jax.ShapeDtypeStruct((p, D_MODEL), F32),
        compiler_params=_cparams(("arbitrary",)),
        name="routed_experts",
    )(tile_expert, xs, wts, w_gate, w_up, w_down)


def _final_kernel(x_ref, h_ref, r_ref, mod_ref, g_ref, wg_ref, wu_ref, wd_ref, out_ref):
    h = h_ref[...]
    hid = (_silu(_dot(h, wg_ref[...])) * _dot(h, wu_ref[...])).astype(BF16)
    y = _dot(hid, wd_ref[...]) + r_ref[...]
    out_ref[...] = x_ref[...] + mod_ref[5:6, :] * _rms(y, g_ref[...])


def _final(x1, h2, routed, mod3, g3, ws_gate, ws_up, ws_down):
    tm = 256
    t = x1.shape[0]
    full = pl.BlockSpec((tm, D_MODEL), lambda i: (i, 0))
    return pl.pallas_call(
        _final_kernel,
        grid=(t // tm,),
        in_specs=[full, full, full,
                  pl.BlockSpec((None, 6, D_MODEL), lambda i: (i // (SEQ // tm), 0, 0)),
                  pl.BlockSpec((1, D_MODEL), lambda i: (0, 0)),
                  pl.BlockSpec((D_MODEL, D_EXPERT), lambda i: (0, 0)),
                  pl.BlockSpec((D_MODEL, D_EXPERT), lambda i: (0, 0)),
                  pl.BlockSpec((D_EXPERT, D_MODEL), lambda i: (0, 0))],
        out_specs=full,
        out_shape=jax.ShapeDtypeStruct((t, D_MODEL), F32),
        compiler_params=_cparams(("arbitrary",)),
        name="shared_expert_final",
    )(x1, h2, routed, mod3, g3, ws_gate, ws_up, ws_down)


def _route(scores, router_bias):
    t = scores.shape[0]
    sel = scores + router_bias.astype(F32)
    gscore = lax.top_k(sel.reshape(t, N_GROUPS, N_EXPERTS // N_GROUPS), 2)[0].sum(-1)
    _, gidx = lax.top_k(gscore, TOPK_GROUPS)
    gmask = jnp.zeros((t, N_GROUPS), dtype=bool).at[jnp.arange(t)[:, None], gidx].set(True)
    emask = jnp.repeat(gmask, N_EXPERTS // N_GROUPS, axis=1)
    _, eidx = lax.top_k(jnp.where(emask, sel, -jnp.inf), TOP_K)
    g = jnp.take_along_axis(scores, eidx, axis=1)
    g = g / g.sum(-1, keepdims=True) * ROUTED_SCALE
    return eidx, g


def _dispatch(eidx, g):
    t = eidx.shape[0]
    a = t * TOP_K
    tm = MOE_TILE
    nb = (a + N_EXPERTS * (tm - 1) + tm - 1) // tm
    p = nb * tm
    e_flat = eidx.reshape(-1)
    t_flat = jnp.repeat(jnp.arange(t, dtype=jnp.int32), TOP_K)
    order = jnp.argsort(e_flat)
    e_s, t_s, g_s = e_flat[order], t_flat[order], g.reshape(-1)[order]
    counts = jnp.bincount(e_flat, length=N_EXPERTS)
    starts = jnp.cumsum(counts) - counts
    padded = (counts + tm - 1) // tm * tm
    pends = jnp.cumsum(padded)
    pstarts = pends - padded
    dest = (pstarts[e_s] + (jnp.arange(a) - starts[e_s])).astype(jnp.int32)
    buf_tok = jnp.zeros((p,), jnp.int32).at[dest].set(t_s)
    buf_w = jnp.zeros((p,), F32).at[dest].set(g_s)
    tile_expert = jnp.clip(jnp.searchsorted(pends, jnp.arange(nb) * tm, side="right"),
                           0, N_EXPERTS - 1).astype(jnp.int32)
    pos = jnp.zeros((a,), jnp.int32).at[order].set(dest)
    return buf_tok, buf_w, tile_expert, pos


def kernel(x, c, w_ada, b_ada, norm_gains, w_in, b_if, conv_w, conv_b, head_gains, rel_bias, w_out,
           w_router, router_bias, w_exp_gate, w_exp_up, w_exp_down, w_sh_gate, w_sh_up, w_sh_down):
    b, s, d = x.shape
    t = b * s
    x2 = x.reshape(t, d)
    for l in range(w_ada.shape[0]):
        c_pad = jnp.pad(c, ((0, 8 - b), (0, 0)))
        mod = _ada(c_pad, w_ada[l], b_ada[l][None, :])[:b]
        mod3 = mod.reshape(b, 6, d)
        g = norm_gains[l]
        w_gates = jnp.pad(w_in[l][:, D_PROJ_MAIN:], ((0, 0), (0, LANES - N_GATES)))
        bif = jnp.pad(b_if[l][None, :], ((0, 0), (0, LANES - N_GATES)))
        proj, gates = _inproj(x2, mod3, g[0][None, :], w_in[l], w_gates, bif)
        gates_t = gates[:, :N_GATES].reshape(b, s, N_GATES).transpose(0, 2, 1)
        hm = _mlstm(proj, gates, gates_t, conv_w[l], conv_b[l][None, :])
        branches = [_attn_branch(proj, _attn_bias(rel_bias, dil), dil) for _, dil in DILATED_PATTERNS]
        (o1, l1), (o2, l2), (o3, l3) = branches
        x1 = _outproj(hm, proj, o1, o2, o3, l1, l2, l3, x2, mod3, head_gains[l][None, :], g[1][None, :],
                      w_out[l].astype(BF16))
        h2, scores = _router(x1, mod3, g[2][None, :], w_router[l])
        eidx, gate_w = _route(scores, router_bias[l])
        buf_tok, buf_w, tile_expert, pos = _dispatch(eidx, gate_w)
        xs = jnp.take(h2, buf_tok, axis=0)
        ys = _experts(tile_expert, xs, buf_w[:, None], w_exp_gate[l], w_exp_up[l], w_exp_down[l])
        routed = jnp.take(ys, pos, axis=0).reshape(t, TOP_K, d).sum(axis=1)
        x2 = _final(x1, h2, routed, mod3, g[3][None, :], w_sh_gate[l].astype(BF16), w_sh_up[l].astype(BF16),
                    w_sh_down[l].astype(BF16))
    return x2.reshape(b, s, d)
```

```python
import numpy as np
import jax
import jax.numpy as jnp
from jax import lax
from jax.experimental import pallas as pl
from jax.experimental.pallas import tpu as pltpu

D_MODEL = 2048
SEQ = 2048
HEAD_DIM = 128
M_HEADS = 8
A_HEADS = 8
D_M = M_HEADS * HEAD_DIM
D_A = A_HEADS * HEAD_DIM
CONV_K = 4
MLSTM_CHUNK = 128
DILATED_PATTERNS = ((128, 1), (512, 4), (2048, 16))
REL_BUCKETS = 32
REL_MAX_DIST = 2048
N_EXPERTS = 256
TOP_K = 8
N_GROUPS = 8
TOPK_GROUPS = 4
D_EXPERT = 512
ROUTED_SCALE = 2.5
NORM_EPS = 1e-6
D_PROJ_MAIN = 4 * D_M + 3 * D_A
N_GATES = 2 * M_HEADS
LANES = 128
ATT_W = 128
MOE_TILE = 128
VMEM_LIMIT = 56 * 1024 * 1024

F32 = jnp.float32
BF16 = jnp.bfloat16
NEG_INF = float("-inf")


def _cparams(sem):
    return pltpu.CompilerParams(dimension_semantics=sem, vmem_limit_bytes=VMEM_LIMIT)


def _dot(a, b):
    return jnp.dot(a, b, preferred_element_type=F32)


def _dot_nt(a, b):
    return lax.dot_general(a, b, (((1,), (1,)), ((), ())), preferred_element_type=F32)


def _dot_tn(a, b):
    return lax.dot_general(a, b, (((0,), (0,)), ((), ())), preferred_element_type=F32)


def _rms(x, g):
    ms = jnp.mean(x * x, axis=-1, keepdims=True)
    return x * lax.rsqrt(ms + NORM_EPS) * g


def _silu(x):
    return x * jax.nn.sigmoid(x)


def _ada_kernel(c_ref, w_ref, b_ref, o_ref):
    a = _silu(c_ref[...]).astype(BF16)
    o_ref[...] = _dot(a, w_ref[...].astype(BF16)) + b_ref[...]


def _ada(c_pad, w_ada, b_ada):
    tn = 1024
    n = w_ada.shape[1]
    return pl.pallas_call(
        _ada_kernel,
        grid=(n // tn,),
        in_specs=[pl.BlockSpec((8, D_MODEL), lambda j: (0, 0)),
                  pl.BlockSpec((D_MODEL, tn), lambda j: (0, j)),
                  pl.BlockSpec((1, tn), lambda j: (0, j))],
        out_specs=pl.BlockSpec((8, tn), lambda j: (0, j)),
        out_shape=jax.ShapeDtypeStruct((8, n), F32),
        compiler_params=_cparams(("arbitrary",)),
        name="ada_mod",
    )(c_pad, w_ada, b_ada)


def _inproj_kernel(x_ref, mod_ref, g_ref, w_ref, wg_ref, bif_ref, proj_ref, gates_ref, h_sc):
    @pl.when(pl.program_id(1) == 0)
    def _():
        h = _rms(x_ref[...], g_ref[...]) * (1.0 + mod_ref[1:2, :]) + mod_ref[0:1, :]
        hb = h.astype(BF16)
        h_sc[...] = hb
        g = _dot(hb, wg_ref[...].astype(BF16)) + bif_ref[...]
        lane = lax.broadcasted_iota(jnp.int32, g.shape, 1)
        logf = jnp.minimum(g, 0.0) - jnp.log1p(jnp.exp(-jnp.abs(g)))
        gates_ref[...] = jnp.where(lane < M_HEADS, g, logf)

    proj_ref[...] = _dot(h_sc[...], w_ref[...].astype(BF16))


def _inproj(x2, mod3, g0, w_in, w_gates, bif):
    tm, tn = 1024, 512
    t = x2.shape[0]
    return pl.pallas_call(
        _inproj_kernel,
        grid=(t // tm, D_PROJ_MAIN // tn),
        in_specs=[pl.BlockSpec((tm, D_MODEL), lambda i, j: (i, 0)),
                  pl.BlockSpec((None, 6, D_MODEL), lambda i, j: (i // (SEQ // tm), 0, 0)),
                  pl.BlockSpec((1, D_MODEL), lambda i, j: (0, 0)),
                  pl.BlockSpec((D_MODEL, tn), lambda i, j: (0, j)),
                  pl.BlockSpec((D_MODEL, LANES), lambda i, j: (0, 0)),
                  pl.BlockSpec((1, LANES), lambda i, j: (0, 0))],
        out_specs=[pl.BlockSpec((tm, tn), lambda i, j: (i, j)),
                   pl.BlockSpec((tm, LANES), lambda i, j: (i, 0))],
        out_shape=[jax.ShapeDtypeStruct((t, D_PROJ_MAIN), F32),
                   jax.ShapeDtypeStruct((t, LANES), F32)],
        scratch_shapes=[pltpu.VMEM((tm, D_MODEL), BF16)],
        compiler_params=_cparams(("arbitrary", "arbitrary")),
        name="in_proj",
    )(x2, mod3, g0, w_in, w_gates, bif)


def _mlstm_kernel(q_ref, k_ref, v_ref, g_ref, gt_ref, cwq_ref, cwk_ref, cbq_ref, cbk_ref, o_ref,
                  ct_sc, n_sc, m_sc, pq_sc, pk_sc):
    lc = MLSTM_CHUNK

    @pl.when(pl.program_id(1) == 0)
    def _():
        ct_sc[...] = jnp.zeros_like(ct_sc)
        n_sc[...] = jnp.zeros_like(n_sc)
        m_sc[...] = jnp.zeros_like(m_sc)
        pq_sc[...] = jnp.zeros_like(pq_sc)
        pk_sc[...] = jnp.zeros_like(pk_sc)

    row_w = lax.broadcasted_iota(jnp.int32, (lc, D_M), 0)

    def conv(cur, prev, w_ref, b_ref):
        acc = cur * w_ref[CONV_K - 1:CONV_K, :] + b_ref[...]
        for s in range(1, CONV_K):
            sh = jnp.where(row_w >= s, pltpu.roll(cur, s, 0), pltpu.roll(prev, s, 0))
            acc = acc + sh * w_ref[CONV_K - 1 - s:CONV_K - s, :]
        return _silu(acc)

    q_raw = q_ref[...]
    k_raw = k_ref[...]
    q_all = conv(q_raw, pq_sc[...], cwq_ref, cbq_ref)
    k_all = conv(k_raw, pk_sc[...], cwk_ref, cbk_ref) * (HEAD_DIM ** -0.5)
    pq_sc[...] = q_raw
    pk_sc[...] = k_raw

    row = lax.broadcasted_iota(jnp.int32, (lc, lc), 0)
    col = lax.broadcasted_iota(jnp.int32, (lc, lc), 1)
    causal = col <= row
    tril = causal.astype(F32)
    triu = (row <= col).astype(F32)
    g = g_ref[...]
    gt = gt_ref[...]
    b_cols = jnp.dot(tril, g, precision=lax.Precision.HIGHEST, preferred_element_type=F32)
    b_rows = jnp.dot(gt, triu, precision=lax.Precision.HIGHEST, preferred_element_type=F32)

    for h in range(M_HEADS):
        sl = slice(h * HEAD_DIM, (h + 1) * HEAD_DIM)
        i_col = g[:, h:h + 1]
        b_col = b_cols[:, M_HEADS + h:M_HEADS + h + 1]
        i_row = gt[h:h + 1, :]
        b_row = b_rows[M_HEADS + h:M_HEADS + h + 1, :]
        m_prev = m_sc[h:h + 1, 0:1]
        n_prev = n_sc[h:h + 1, :]
        ct_prev = ct_sc[h]
        qh = q_all[:, sl]
        kh = k_all[:, sl]
        vh = v_ref[:, sl]
        qb = qh.astype(BF16)
        kb = kh.astype(BF16)

        dmat = jnp.where(causal, b_col - b_row + i_row, NEG_INF)
        inter = b_col + m_prev
        m_t = jnp.maximum(inter, jnp.max(dmat, axis=-1, keepdims=True))
        w_intra = jnp.exp(dmat - m_t)
        w_inter = jnp.exp(inter - m_t)
        s = _dot_nt(qb, kb) * w_intra
        num = _dot(s.astype(BF16), vh.astype(BF16)) + w_inter * _dot(qb, ct_prev.astype(BF16))
        den = jnp.sum(s, axis=-1, keepdims=True) + w_inter * jnp.sum(qh * n_prev, axis=-1, keepdims=True)
        o_ref[:, sl] = num / jnp.maximum(jnp.abs(den), jnp.exp(-m_t))

        b_last = b_col[lc - 1:lc, :]
        log_w = b_last - b_col + i_col
        m_new = jnp.maximum(b_last + m_prev, jnp.max(log_w, axis=0, keepdims=True))
        w_upd = jnp.exp(log_w - m_new)
        decay = jnp.exp(b_last + m_prev - m_new)
        ct_sc[h] = decay * ct_prev + _dot_tn(kb, (vh * w_upd).astype(BF16))
        n_sc[h:h + 1, :] = decay * n_prev + jnp.sum(kh * w_upd, axis=0, keepdims=True)
        m_sc[h:h + 1, :] = jnp.broadcast_to(m_new, (1, LANES))


def _mlstm(proj, gates, gates_t, conv_w, conv_b):
    lc = MLSTM_CHUNK
    nc = SEQ // lc
    t = proj.shape[0]
    return pl.pallas_call(
        _mlstm_kernel,
        grid=(t // SEQ, nc),
        in_specs=[pl.BlockSpec((lc, D_M), lambda b, c: (b * nc + c, 0)),
                  pl.BlockSpec((lc, D_M), lambda b, c: (b * nc + c, 1)),
                  pl.BlockSpec((lc, D_M), lambda b, c: (b * nc + c, 2)),
                  pl.BlockSpec((lc, LANES), lambda b, c: (b * nc + c, 0)),
                  pl.BlockSpec((None, N_GATES, lc), lambda b, c: (b, 0, c)),
                  pl.BlockSpec((CONV_K, D_M), lambda b, c: (0, 0)),
                  pl.BlockSpec((CONV_K, D_M), lambda b, c: (0, 1)),
                  pl.BlockSpec((1, D_M), lambda b, c: (0, 0)),
                  pl.BlockSpec((1, D_M), lambda b, c: (0, 1))],
        out_specs=pl.BlockSpec((lc, D_M), lambda b, c: (b * nc + c, 0)),
        out_shape=jax.ShapeDtypeStruct((t, D_M), F32),
        scratch_shapes=[pltpu.VMEM((M_HEADS, HEAD_DIM, HEAD_DIM), F32),
                        pltpu.VMEM((M_HEADS, HEAD_DIM), F32),
                        pltpu.VMEM((M_HEADS, LANES), F32),
                        pltpu.VMEM((lc, D_M), F32),
                        pltpu.VMEM((lc, D_M), F32)],
        compiler_params=_cparams(("arbitrary", "arbitrary")),
        name="mlstm",
    )(proj, proj, proj, gates, gates_t, conv_w, conv_w, conv_b, conv_b)


def _attn_kernel(q_ref, k_ref, v_ref, bias_ref, out_ref, o_sc, l_sc):
    w = ATT_W
    row = lax.broadcasted_iota(jnp.int32, (w, w), 0)
    col = lax.broadcasted_iota(jnp.int32, (w, w), 1)
    cur_ok = col <= row
    scale = HEAD_DIM ** -0.5
    for p, (_, d) in enumerate(DILATED_PATTERNS):
        nb = SEQ // d // w
        nb_shift = nb.bit_length() - 1

        def rows(start, d=d):
            return pl.ds(start, w) if d == 1 else pl.ds(start, w, stride=d)

        def block(i, carry, p=p, d=d, nb=nb, nb_shift=nb_shift, rows=rows):
            r = lax.shift_right_logical(i, nb_shift)
            n = jnp.bitwise_and(i, nb - 1)
            start = r + n * (w * d)
            qb = q_ref[rows(start), :].astype(BF16)
            s_c = _dot_nt(qb, k_ref[rows(start), :].astype(BF16)) * scale + bias_ref[p, :, w:2 * w]
            s_c = jnp.where(cur_ok, s_c, NEG_INF)
            m = jnp.max(s_c, axis=-1, keepdims=True)
            if nb > 1:
                pstart = r + jnp.maximum(n - 1, 0) * (w * d)
                s_p = _dot_nt(qb, k_ref[rows(pstart), :].astype(BF16)) * scale + bias_ref[p, :, 0:w]
                s_p = jnp.where((col >= row) & (n > 0), s_p, NEG_INF)
                m = jnp.maximum(m, jnp.max(s_p, axis=-1, keepdims=True))
            p_c = jnp.exp(s_c - m)
            den = jnp.sum(p_c, axis=-1, keepdims=True)
            acc = _dot(p_c.astype(BF16), v_ref[rows(start), :].astype(BF16))
            if nb > 1:
                p_p = jnp.exp(s_p - m)
                den = den + jnp.sum(p_p, axis=-1, keepdims=True)
                acc = acc + _dot(p_p.astype(BF16), v_ref[rows(pstart), :].astype(BF16))
            o_sc[p, rows(start), :] = acc / den
            l_sc[p, rows(start), :] = jnp.broadcast_to(m + jnp.log(den), (w, HEAD_DIM))
            return carry

        lax.fori_loop(0, d * nb, block, 0)

    chunk = 256

    def combine(c, carry):
        rs = pl.ds(pl.multiple_of(c * chunk, chunk), chunk)
        l1, l2, l3 = l_sc[0, rs, :], l_sc[1, rs, :], l_sc[2, rs, :]
        lm = jnp.maximum(jnp.maximum(l1, l2), l3)
        e1, e2, e3 = jnp.exp(l1 - lm), jnp.exp(l2 - lm), jnp.exp(l3 - lm)
        tot = e1 + e2 + e3
        out_ref[rs, :] = (e1 / tot) * o_sc[0, rs, :] + (e2 / tot) * o_sc[1, rs, :] + (e3 / tot) * o_sc[2, rs, :]
        return carry

    lax.fori_loop(0, SEQ // chunk, combine, 0)


def _attn(proj, bias_all):
    t = proj.shape[0]
    nblk = D_PROJ_MAIN // HEAD_DIM
    qi, ki, vi = nblk - 3 * A_HEADS, nblk - 2 * A_HEADS, nblk - A_HEADS
    npat = len(DILATED_PATTERNS)

    def head(base):
        return pl.BlockSpec((SEQ, HEAD_DIM), lambda b, h: (b, base + h))

    return pl.pallas_call(
        _attn_kernel,
        grid=(t // SEQ, A_HEADS),
        in_specs=[head(qi), head(ki), head(vi),
                  pl.BlockSpec((npat, None, ATT_W, 2 * ATT_W), lambda b, h: (0, h, 0, 0))],
        out_specs=pl.BlockSpec((SEQ, HEAD_DIM), lambda b, h: (b, h)),
        out_shape=jax.ShapeDtypeStruct((t, D_A), F32),
        scratch_shapes=[pltpu.VMEM((npat, SEQ, HEAD_DIM), F32),
                        pltpu.VMEM((npat, SEQ, HEAD_DIM), F32)],
        compiler_params=_cparams(("arbitrary", "arbitrary")),
        name="dilated_attn",
    )(proj, proj, proj, bias_all)


def _t5_causal_bucket(dist):
    max_exact = REL_BUCKETS // 2
    d = np.maximum(dist, 1).astype(np.float32)
    large = max_exact + (np.log(d / max_exact) / np.log(REL_MAX_DIST / max_exact)
                         * (REL_BUCKETS - max_exact)).astype(np.int32)
    return np.where(dist < max_exact, dist, np.minimum(large, REL_BUCKETS - 1)).astype(np.int32)


def _attn_bias(rel_bias):
    w = ATT_W
    j = w + np.arange(w)[:, None] - np.arange(2 * w)[None, :]
    tabs = [rel_bias[_t5_causal_bucket(np.clip(j, 0, w) * dil)] for _, dil in DILATED_PATTERNS]
    return jnp.transpose(jnp.stack(tabs), (0, 3, 1, 2)).astype(F32)


def _head_rms(x, g_ref, base):
    outs = []
    for h in range(x.shape[1] // HEAD_DIM):
        sl = slice(h * HEAD_DIM, (h + 1) * HEAD_DIM)
        outs.append(_rms(x[:, sl], g_ref[:, base + h * HEAD_DIM:base + (h + 1) * HEAD_DIM]))
    return outs


def _outproj_kernel(hm_ref, om_ref, ha_ref, x_ref, mod_ref, hg_ref, g_ref, w_ref, out_ref, cat_sc):
    gate = jax.nn.sigmoid(om_ref[...])
    for h, seg in enumerate(_head_rms(hm_ref[...], hg_ref, 0)):
        sl = slice(h * HEAD_DIM, (h + 1) * HEAD_DIM)
        cat_sc[:, sl] = (seg * gate[:, sl]).astype(BF16)
    for h, seg in enumerate(_head_rms(ha_ref[...], hg_ref, D_M)):
        cat_sc[:, D_M + h * HEAD_DIM:D_M + (h + 1) * HEAD_DIM] = seg.astype(BF16)
    y = _dot(cat_sc[...], w_ref[...])
    out_ref[...] = x_ref[...] + mod_ref[2:3, :] * _rms(y, g_ref[...])


def _outproj(hm, proj, ha, x2, mod3, head_gains, g1, w_out_bf):
    tm = 256
    t = x2.shape[0]
    half = lambda blk: pl.BlockSpec((tm, D_M), lambda i: (i, blk))
    full = pl.BlockSpec((tm, D_MODEL), lambda i: (i, 0))
    vec = pl.BlockSpec((1, D_MODEL), lambda i: (0, 0))
    return pl.pallas_call(
        _outproj_kernel,
        grid=(t // tm,),
        in_specs=[half(0), half(3), half(0), full,
                  pl.BlockSpec((None, 6, D_MODEL), lambda i: (i // (SEQ // tm), 0, 0)),
                  vec, vec,
                  pl.BlockSpec((D_MODEL, D_MODEL), lambda i: (0, 0))],
        out_specs=full,
        out_shape=jax.ShapeDtypeStruct((t, D_MODEL), F32),
        scratch_shapes=[pltpu.VMEM((tm, D_MODEL), BF16)],
        compiler_params=_cparams(("arbitrary",)),
        name="out_proj",
    )(hm, proj, ha, x2, mod3, head_gains, g1, w_out_bf)


def _router_kernel(x_ref, mod_ref, g_ref, w_ref, rb_ref, h_ref, idx_ref, gate_ref):
    tm = x_ref.shape[0]
    h = _rms(x_ref[...], g_ref[...]) * (1.0 + mod_ref[4:5, :]) + mod_ref[3:4, :]
    hb = h.astype(BF16)
    h_ref[...] = hb
    scores = jax.nn.sigmoid(_dot(hb, w_ref[...].astype(BF16))).T
    sel = scores + rb_ref[...]
    ge = N_EXPERTS // N_GROUPS
    grow = lax.broadcasted_iota(jnp.int32, (N_GROUPS, tm), 0)
    gscore = jnp.zeros((N_GROUPS, tm), F32)
    for g in range(N_GROUPS):
        blk = sel[g * ge:(g + 1) * ge, :]
        m1 = jnp.max(blk, axis=0, keepdims=True)
        is1 = blk == m1
        cnt = jnp.sum(is1.astype(F32), axis=0, keepdims=True)
        m2 = jnp.max(jnp.where(is1, NEG_INF, blk), axis=0, keepdims=True)
        gscore = jnp.where(grow == g, m1 + jnp.where(cnt >= 2.0, m1, m2), gscore)
    rank = jnp.zeros((N_GROUPS, tm), F32)
    for g in range(N_GROUPS):
        other = gscore[g:g + 1, :]
        beats = (other > gscore) | ((other == gscore) & (grow > g))
        rank = rank + beats.astype(F32)
    keep = rank < float(TOPK_GROUPS)
    xm = jnp.concatenate([jnp.where(keep[g:g + 1, :], sel[g * ge:(g + 1) * ge, :], NEG_INF)
                          for g in range(N_GROUPS)], axis=0)
    erow = lax.broadcasted_iota(jnp.int32, (N_EXPERTS, tm), 0).astype(F32)
    krow = lax.broadcasted_iota(jnp.int32, (TOP_K, tm), 0)
    idx = jnp.zeros((TOP_K, tm), F32)
    gk = jnp.zeros((TOP_K, tm), F32)
    for k in range(TOP_K):
        m = jnp.max(xm, axis=0, keepdims=True)
        first = jnp.min(jnp.where(xm == m, erow, float(N_EXPERTS)), axis=0, keepdims=True)
        hit = erow == first
        idx = jnp.where(krow == k, first, idx)
        gk = jnp.where(krow == k, jnp.sum(jnp.where(hit, scores, 0.0), axis=0, keepdims=True), gk)
        xm = jnp.where(hit, NEG_INF, xm)
    idx_ref[...] = idx.astype(jnp.int32)
    gate_ref[...] = gk / jnp.sum(gk, axis=0, keepdims=True) * ROUTED_SCALE


def _router(x1, mod3, g2, w_router, rb_col):
    tm = 256
    t = x1.shape[0]
    return pl.pallas_call(
        _router_kernel,
        grid=(t // tm,),
        in_specs=[pl.BlockSpec((tm, D_MODEL), lambda i: (i, 0)),
                  pl.BlockSpec((None, 6, D_MODEL), lambda i: (i // (SEQ // tm), 0, 0)),
                  pl.BlockSpec((1, D_MODEL), lambda i: (0, 0)),
                  pl.BlockSpec((D_MODEL, N_EXPERTS), lambda i: (0, 0)),
                  pl.BlockSpec((N_EXPERTS, 1), lambda i: (0, 0))],
        out_specs=[pl.BlockSpec((tm, D_MODEL), lambda i: (i, 0)),
                   pl.BlockSpec((TOP_K, tm), lambda i: (0, i)),
                   pl.BlockSpec((TOP_K, tm), lambda i: (0, i))],
        out_shape=[jax.ShapeDtypeStruct((t, D_MODEL), BF16),
                   jax.ShapeDtypeStruct((TOP_K, t), jnp.int32),
                   jax.ShapeDtypeStruct((TOP_K, t), F32)],
        compiler_params=_cparams(("arbitrary",)),
        name="ffn_norm_router",
    )(x1, mod3, g2, w_router, rb_col)


def _experts_kernel(te_ref, first_ref, nxt_ref, par_ref, valid_ref, dst_ref, x_ref, wt_ref,
                    wg_hbm, wu_hbm, wd_hbm, y_hbm,
                    wg_f, wu_f, wd_f, wg_b, wu_b, wd_b, ybuf, wsem, ysem):
    tm = MOE_TILE
    i = pl.program_id(0)
    last = pl.num_programs(0) - 1
    slot = par_ref[i]
    ys = i % 2

    def w_copies(e, s):
        return (pltpu.make_async_copy(wg_hbm.at[e], wg_f.at[s], wsem.at[s, 0]),
                pltpu.make_async_copy(wu_hbm.at[e], wu_f.at[s], wsem.at[s, 1]),
                pltpu.make_async_copy(wd_hbm.at[e], wd_f.at[s], wsem.at[s, 2]))

    def y_copy(r, s, row):
        return pltpu.make_async_copy(ybuf.at[s, pl.ds(r, 1)], y_hbm.at[pl.ds(row, 1)], ysem.at[s])

    def wait_rows(s):
        for r in range(tm):
            y_copy(r, s, 0).wait()

    @pl.when(i == 0)
    def _():
        for cp in w_copies(te_ref[0], 0):
            cp.start()
        ybuf[1] = jnp.zeros((tm, D_MODEL), F32)
        for blk in range(2):
            cp = pltpu.make_async_copy(ybuf.at[1], y_hbm.at[pl.ds(y_hbm.shape[0] - (blk + 1) * tm, tm)], ysem.at[1])
            cp.start()
            cp.wait()

    @pl.when(first_ref[i] == 1)
    def _():
        for cp in w_copies(te_ref[i], slot):
            cp.wait()

        @pl.when(nxt_ref[i] >= 0)
        def _():
            for cp in w_copies(nxt_ref[i], 1 - slot):
                cp.start()

        wg_b[...] = wg_f[slot].astype(BF16)
        wu_b[...] = wu_f[slot].astype(BF16)
        wd_b[...] = wd_f[slot].astype(BF16)

    @pl.when((i >= 2) & (valid_ref[jnp.maximum(i - 2, 0)] == 1))
    def _():
        wait_rows(ys)

    @pl.when(valid_ref[i] == 1)
    def _():
        x = x_ref[...]
        hid = (_silu(_dot(x, wg_b[...])) * _dot(x, wu_b[...])).astype(BF16)
        ybuf[ys] = _dot(hid, wd_b[...]) * wt_ref[...]
        for r in range(tm):
            y_copy(r, ys, dst_ref[0, r]).start()

    @pl.when(i == last)
    def _():
        @pl.when((i >= 1) & (valid_ref[jnp.maximum(i - 1, 0)] == 1))
        def _():
            wait_rows(1 - ys)

        @pl.when(valid_ref[i] == 1)
        def _():
            wait_rows(ys)


def _experts(sched, dst, xs, wts, w_gate, w_up, w_down, n_rows_out):
    tm = MOE_TILE
    nt = dst.shape[0]
    any_spec = pl.BlockSpec(memory_space=pl.ANY)
    grid_spec = pltpu.PrefetchScalarGridSpec(
        num_scalar_prefetch=len(sched),
        grid=(nt,),
        in_specs=[pl.BlockSpec((None, 1, tm), lambda i, *_: (i, 0, 0), memory_space=pltpu.SMEM),
                  pl.BlockSpec((tm, D_MODEL), lambda i, *_: (i, 0)),
                  pl.BlockSpec((tm, 1), lambda i, *_: (i, 0)),
                  any_spec, any_spec, any_spec],
        out_specs=any_spec,
        scratch_shapes=[pltpu.VMEM((2, D_MODEL, D_EXPERT), F32),
                        pltpu.VMEM((2, D_MODEL, D_EXPERT), F32),
                        pltpu.VMEM((2, D_EXPERT, D_MODEL), F32),
                        pltpu.VMEM((D_MODEL, D_EXPERT), BF16),
                        pltpu.VMEM((D_MODEL, D_EXPERT), BF16),
                        pltpu.VMEM((D_EXPERT, D_MODEL), BF16),
                        pltpu.VMEM((2, tm, D_MODEL), F32),
                        pltpu.SemaphoreType.DMA((2, 3)),
                        pltpu.SemaphoreType.DMA((2,))],
    )
    return pl.pallas_call(
        _experts_kernel,
        grid_spec=grid_spec,
        out_shape=jax.ShapeDtypeStruct((n_rows_out, D_MODEL), F32),
        compiler_params=_cparams(("arbitrary",)),
        name="routed_experts",
    )(*sched, dst, xs, wts, w_gate, w_up, w_down)


def _final_kernel(x_ref, h_ref, *rest):
    y_refs = rest[:TOP_K]
    mod_ref, g_ref, wg_ref, wu_ref, wd_ref, out_ref = rest[TOP_K:]
    h = h_ref[...]
    hid = (_silu(_dot(h, wg_ref[...])) * _dot(h, wu_ref[...])).astype(BF16)
    y = _dot(hid, wd_ref[...])
    for r in y_refs:
        y = y + r[...]
    out_ref[...] = x_ref[...] + mod_ref[5:6, :] * _rms(y, g_ref[...])


def _final(x1, h2, y_rows, mod3, g3, ws_gate, ws_up, ws_down):
    tm = 128
    t = x1.shape[0]
    full = pl.BlockSpec((tm, D_MODEL), lambda i: (i, 0))

    def slab(k):
        return pl.BlockSpec((tm, D_MODEL), lambda i: (k * (t // tm) + i, 0))

    return pl.pallas_call(
        _final_kernel,
        grid=(t // tm,),
        in_specs=[full, full] + [slab(k) for k in range(TOP_K)] + [
            pl.BlockSpec((None, 6, D_MODEL), lambda i: (i // (SEQ // tm), 0, 0)),
            pl.BlockSpec((1, D_MODEL), lambda i: (0, 0)),
            pl.BlockSpec((D_MODEL, D_EXPERT), lambda i: (0, 0)),
            pl.BlockSpec((D_MODEL, D_EXPERT), lambda i: (0, 0)),
            pl.BlockSpec((D_EXPERT, D_MODEL), lambda i: (0, 0))],
        out_specs=full,
        out_shape=jax.ShapeDtypeStruct((t, D_MODEL), F32),
        compiler_params=_cparams(("arbitrary",)),
        name="shared_expert_final",
    )(x1, h2, *([y_rows] * TOP_K), mod3, g3, ws_gate, ws_up, ws_down)


def _dispatch(idx_t, gate_t):
    t = idx_t.shape[1]
    a = t * TOP_K
    tm = MOE_TILE
    nt = (a + N_EXPERTS * (tm - 1) + tm - 1) // tm
    e_flat = idx_t.reshape(-1)
    order = jnp.argsort(e_flat).astype(jnp.int32)
    e_sorted = e_flat[order]
    experts = jnp.arange(N_EXPERTS, dtype=jnp.int32)
    starts = jnp.searchsorted(e_sorted, experts, side="left").astype(jnp.int32)
    counts = jnp.searchsorted(e_sorted, experts, side="right").astype(jnp.int32) - starts
    padded = (counts + tm - 1) // tm * tm
    pends = jnp.cumsum(padded)
    pstarts = pends - padded
    tile_start = jnp.arange(nt, dtype=jnp.int32) * tm
    te = jnp.clip(jnp.searchsorted(pends, tile_start, side="right"), 0, N_EXPERTS - 1).astype(jnp.int32)
    valid = tile_start < pends[-1]
    lane = jnp.arange(tm, dtype=jnp.int32)[None, :]
    j = tile_start[:, None] + lane - pstarts[te][:, None]
    row_ok = (j < counts[te][:, None]) & valid[:, None]
    asg = order[jnp.clip(starts[te][:, None] + j, 0, a - 1)]
    tiles = jnp.arange(nt, dtype=jnp.int32)
    dst = jnp.where(row_ok, asg, a + (tiles % 2)[:, None] * tm + lane).astype(jnp.int32)
    tok = jnp.where(row_ok, asg % t, 0).astype(jnp.int32)
    wt = jnp.where(row_ok, gate_t.reshape(-1)[asg], 0.0)
    first = jnp.concatenate([jnp.ones((1,), bool), te[1:] != te[:-1]])
    par = ((jnp.cumsum(first.astype(jnp.int32)) - 1) % 2).astype(jnp.int32)
    run_start = jnp.where(first, tiles, nt)
    next_start = jnp.concatenate([lax.cummin(run_start[::-1])[::-1][1:], jnp.full((1,), nt, jnp.int32)])
    nxt = jnp.where(next_start < nt, te[jnp.clip(next_start, 0, nt - 1)], -1).astype(jnp.int32)
    sched = (te, first.astype(jnp.int32), nxt, par, valid.astype(jnp.int32))
    return sched, dst[:, None, :], tok.reshape(-1), wt.reshape(-1, 1)


def kernel(x, c, w_ada, b_ada, norm_gains, w_in, b_if, conv_w, conv_b, head_gains, rel_bias, w_out,
           w_router, router_bias, w_exp_gate, w_exp_up, w_exp_down, w_sh_gate, w_sh_up, w_sh_down):
    b, s, d = x.shape
    t = b * s
    x2 = x.reshape(t, d)
    for l in range(w_ada.shape[0]):
        c_pad = jnp.pad(c, ((0, 8 - b), (0, 0)))
        mod = _ada(c_pad, w_ada[l], b_ada[l][None, :])[:b]
        mod3 = mod.reshape(b, 6, d)
        g = norm_gains[l]
        w_gates = jnp.pad(w_in[l][:, D_PROJ_MAIN:], ((0, 0), (0, LANES - N_GATES)))
        bif = jnp.pad(b_if[l][None, :], ((0, 0), (0, LANES - N_GATES)))
        proj, gates = _inproj(x2, mod3, g[0][None, :], w_in[l], w_gates, bif)
        gates_t = gates[:, :N_GATES].reshape(b, s, N_GATES).transpose(0, 2, 1)
        hm = _mlstm(proj, gates, gates_t, conv_w[l], conv_b[l][None, :])
        ha = _attn(proj, _attn_bias(rel_bias))
        x1 = _outproj(hm, proj, ha, x2, mod3, head_gains[l][None, :], g[1][None, :], w_out[l].astype(BF16))
        h2, idx_t, gate_t = _router(x1, mod3, g[2][None, :], w_router[l], router_bias[l][:, None])
        sched, dst, tok, wts = _dispatch(idx_t, gate_t)
        xs = jnp.take(h2, tok, axis=0)
        y_rows = _experts(sched, dst, xs, wts, w_exp_gate[l], w_exp_up[l], w_exp_down[l], t * TOP_K + 2 * MOE_TILE)
        x2 = _final(x1, h2, y_rows, mod3, g[3][None, :], w_sh_gate[l].astype(BF16), w_sh_up[l].astype(BF16),
                    w_sh_down[l].astype(BF16))
    return x2.reshape(b, s, d)
```

```python
import numpy as np
import jax
import jax.numpy as jnp
from jax import lax
from jax.experimental import pallas as pl
from jax.experimental.pallas import tpu as pltpu

D_MODEL = 2048
SEQ = 2048
HEAD_DIM = 128
M_HEADS = 8
A_HEADS = 8
D_M = M_HEADS * HEAD_DIM
D_A = A_HEADS * HEAD_DIM
CONV_K = 4
MLSTM_CHUNK = 128
DILATED_PATTERNS = ((128, 1), (512, 4), (2048, 16))
REL_BUCKETS = 32
REL_MAX_DIST = 2048
N_EXPERTS = 256
TOP_K = 8
N_GROUPS = 8
TOPK_GROUPS = 4
D_EXPERT = 512
ROUTED_SCALE = 2.5
NORM_EPS = 1e-6
D_PROJ_MAIN = 4 * D_M + 3 * D_A
N_GATES = 2 * M_HEADS
LANES = 128
ATT_W = 128
MOE_TILE = 128
VMEM_LIMIT = 56 * 1024 * 1024

F32 = jnp.float32
BF16 = jnp.bfloat16
NEG_INF = float("-inf")


def _cparams(sem):
    return pltpu.CompilerParams(dimension_semantics=sem, vmem_limit_bytes=VMEM_LIMIT)


def _dot(a, b):
    return jnp.dot(a, b, preferred_element_type=F32)


def _dot_nt(a, b):
    return lax.dot_general(a, b, (((1,), (1,)), ((), ())), preferred_element_type=F32)


def _dot_tn(a, b):
    return lax.dot_general(a, b, (((0,), (0,)), ((), ())), preferred_element_type=F32)


def _rms(x, g):
    ms = jnp.mean(x * x, axis=-1, keepdims=True)
    return x * lax.rsqrt(ms + NORM_EPS) * g


def _silu(x):
    return x * jax.nn.sigmoid(x)


def _ada_kernel(c_ref, w_ref, b_ref, o_ref):
    a = _silu(c_ref[...]).astype(BF16)
    o_ref[...] = _dot(a, w_ref[...].astype(BF16)) + b_ref[...]


def _ada(c_pad, w_ada, b_ada):
    tn = 1024
    n = w_ada.shape[1]
    return pl.pallas_call(
        _ada_kernel,
        grid=(n // tn,),
        in_specs=[pl.BlockSpec((8, D_MODEL), lambda j: (0, 0)),
                  pl.BlockSpec((D_MODEL, tn), lambda j: (0, j)),
                  pl.BlockSpec((1, tn), lambda j: (0, j))],
        out_specs=pl.BlockSpec((8, tn), lambda j: (0, j)),
        out_shape=jax.ShapeDtypeStruct((8, n), F32),
        compiler_params=_cparams(("arbitrary",)),
        name="ada_mod",
    )(c_pad, w_ada, b_ada)


def _inproj_kernel(x_ref, mod_ref, g_ref, w_ref, wg_ref, bif_ref, proj_ref, gates_ref, h_sc):
    @pl.when(pl.program_id(1) == 0)
    def _():
        h = _rms(x_ref[...], g_ref[...]) * (1.0 + mod_ref[1:2, :]) + mod_ref[0:1, :]
        hb = h.astype(BF16)
        h_sc[...] = hb
        g = _dot(hb, wg_ref[...].astype(BF16)) + bif_ref[...]
        lane = lax.broadcasted_iota(jnp.int32, g.shape, 1)
        logf = jnp.minimum(g, 0.0) - jnp.log1p(jnp.exp(-jnp.abs(g)))
        gates_ref[...] = jnp.where(lane < M_HEADS, g, logf)

    proj_ref[...] = _dot(h_sc[...], w_ref[...].astype(BF16))


def _inproj(x2, mod3, g0, w_in, w_gates, bif):
    tm, tn = 1024, 512
    t = x2.shape[0]
    return pl.pallas_call(
        _inproj_kernel,
        grid=(t // tm, D_PROJ_MAIN // tn),
        in_specs=[pl.BlockSpec((tm, D_MODEL), lambda i, j: (i, 0)),
                  pl.BlockSpec((None, 6, D_MODEL), lambda i, j: (i // (SEQ // tm), 0, 0)),
                  pl.BlockSpec((1, D_MODEL), lambda i, j: (0, 0)),
                  pl.BlockSpec((D_MODEL, tn), lambda i, j: (0, j)),
                  pl.BlockSpec((D_MODEL, LANES), lambda i, j: (0, 0)),
                  pl.BlockSpec((1, LANES), lambda i, j: (0, 0))],
        out_specs=[pl.BlockSpec((tm, tn), lambda i, j: (i, j)),
                   pl.BlockSpec((tm, LANES), lambda i, j: (i, 0))],
        out_shape=[jax.ShapeDtypeStruct((t, D_PROJ_MAIN), F32),
                   jax.ShapeDtypeStruct((t, LANES), F32)],
        scratch_shapes=[pltpu.VMEM((tm, D_MODEL), BF16)],
        compiler_params=_cparams(("arbitrary", "arbitrary")),
        name="in_proj",
    )(x2, mod3, g0, w_in, w_gates, bif)


def _mlstm_kernel(q_ref, k_ref, v_ref, g_ref, gt_ref, cwq_ref, cwk_ref, cbq_ref, cbk_ref, o_ref,
                  ct_sc, n_sc, m_sc, pq_sc, pk_sc):
    lc = MLSTM_CHUNK

    @pl.when(pl.program_id(1) == 0)
    def _():
        ct_sc[...] = jnp.zeros_like(ct_sc)
        n_sc[...] = jnp.zeros_like(n_sc)
        m_sc[...] = jnp.zeros_like(m_sc)
        pq_sc[...] = jnp.zeros_like(pq_sc)
        pk_sc[...] = jnp.zeros_like(pk_sc)

    row_w = lax.broadcasted_iota(jnp.int32, (lc, D_M), 0)

    def conv(cur, prev, w_ref, b_ref):
        acc = cur * w_ref[CONV_K - 1:CONV_K, :] + b_ref[...]
        for s in range(1, CONV_K):
            sh = jnp.where(row_w >= s, pltpu.roll(cur, s, 0), pltpu.roll(prev, s, 0))
            acc = acc + sh * w_ref[CONV_K - 1 - s:CONV_K - s, :]
        return _silu(acc)

    q_raw = q_ref[...]
    k_raw = k_ref[...]
    q_all = conv(q_raw, pq_sc[...], cwq_ref, cbq_ref)
    k_all = conv(k_raw, pk_sc[...], cwk_ref, cbk_ref) * (HEAD_DIM ** -0.5)
    pq_sc[...] = q_raw
    pk_sc[...] = k_raw

    row = lax.broadcasted_iota(jnp.int32, (lc, lc), 0)
    col = lax.broadcasted_iota(jnp.int32, (lc, lc), 1)
    causal = col <= row
    tril = causal.astype(F32)
    triu = (row <= col).astype(F32)
    g = g_ref[...]
    gt = gt_ref[...]
    b_cols = jnp.dot(tril, g, precision=lax.Precision.HIGHEST, preferred_element_type=F32)
    b_rows = jnp.dot(gt, triu, precision=lax.Precision.HIGHEST, preferred_element_type=F32)

    for h in range(M_HEADS):
        sl = slice(h * HEAD_DIM, (h + 1) * HEAD_DIM)
        i_col = g[:, h:h + 1]
        b_col = b_cols[:, M_HEADS + h:M_HEADS + h + 1]
        i_row = gt[h:h + 1, :]
        b_row = b_rows[M_HEADS + h:M_HEADS + h + 1, :]
        m_prev = m_sc[h:h + 1, 0:1]
        n_prev = n_sc[h:h + 1, :]
        ct_prev = ct_sc[h]
        qh = q_all[:, sl]
        kh = k_all[:, sl]
        vh = v_ref[:, sl]
        qb = qh.astype(BF16)
        kb = kh.astype(BF16)

        dmat = jnp.where(causal, b_col - b_row + i_row, NEG_INF)
        inter = b_col + m_prev
        m_t = jnp.maximum(inter, jnp.max(dmat, axis=-1, keepdims=True))
        w_intra = jnp.exp(dmat - m_t)
        w_inter = jnp.exp(inter - m_t)
        s = _dot_nt(qb, kb) * w_intra
        num = _dot(s.astype(BF16), vh.astype(BF16)) + w_inter * _dot(qb, ct_prev.astype(BF16))
        den = jnp.sum(s, axis=-1, keepdims=True) + w_inter * jnp.sum(qh * n_prev, axis=-1, keepdims=True)
        o_ref[:, sl] = num / jnp.maximum(jnp.abs(den), jnp.exp(-m_t))

        b_last = b_col[lc - 1:lc, :]
        log_w = b_last - b_col + i_col
        m_new = jnp.maximum(b_last + m_prev, jnp.max(log_w, axis=0, keepdims=True))
        w_upd = jnp.exp(log_w - m_new)
        decay = jnp.exp(b_last + m_prev - m_new)
        ct_sc[h] = decay * ct_prev + _dot_tn(kb, (vh * w_upd).astype(BF16))
        n_sc[h:h + 1, :] = decay * n_prev + jnp.sum(kh * w_upd, axis=0, keepdims=True)
        m_sc[h:h + 1, :] = jnp.broadcast_to(m_new, (1, LANES))


def _mlstm(proj, gates, gates_t, conv_w, conv_b):
    lc = MLSTM_CHUNK
    nc = SEQ // lc
    t = proj.shape[0]
    return pl.pallas_call(
        _mlstm_kernel,
        grid=(t // SEQ, nc),
        in_specs=[pl.BlockSpec((lc, D_M), lambda b, c: (b * nc + c, 0)),
                  pl.BlockSpec((lc, D_M), lambda b, c: (b * nc + c, 1)),
                  pl.BlockSpec((lc, D_M), lambda b, c: (b * nc + c, 2)),
                  pl.BlockSpec((lc, LANES), lambda b, c: (b * nc + c, 0)),
                  pl.BlockSpec((None, N_GATES, lc), lambda b, c: (b, 0, c)),
                  pl.BlockSpec((CONV_K, D_M), lambda b, c: (0, 0)),
                  pl.BlockSpec((CONV_K, D_M), lambda b, c: (0, 1)),
                  pl.BlockSpec((1, D_M), lambda b, c: (0, 0)),
                  pl.BlockSpec((1, D_M), lambda b, c: (0, 1))],
        out_specs=pl.BlockSpec((lc, D_M), lambda b, c: (b * nc + c, 0)),
        out_shape=jax.ShapeDtypeStruct((t, D_M), F32),
        scratch_shapes=[pltpu.VMEM((M_HEADS, HEAD_DIM, HEAD_DIM), F32),
                        pltpu.VMEM((M_HEADS, HEAD_DIM), F32),
                        pltpu.VMEM((M_HEADS, LANES), F32),
                        pltpu.VMEM((lc, D_M), F32),
                        pltpu.VMEM((lc, D_M), F32)],
        compiler_params=_cparams(("arbitrary", "arbitrary")),
        name="mlstm",
    )(proj, proj, proj, gates, gates_t, conv_w, conv_w, conv_b, conv_b)


def _attn_kernel(q_ref, k_ref, v_ref, bias_ref, out_ref, o_sc, l_sc, bias_sc):
    w = ATT_W
    row = lax.broadcasted_iota(jnp.int32, (w, w), 0)
    col = lax.broadcasted_iota(jnp.int32, (w, w), 1)
    cur_ok = col <= row
    scale = HEAD_DIM ** -0.5
    for p in range(len(DILATED_PATTERNS)):
        bias_sc[p] = pltpu.roll(jnp.broadcast_to(bias_ref[p], (w, 2 * w)), 0, 1, stride=1, stride_axis=0)
    for p, (_, d) in enumerate(DILATED_PATTERNS):
        nb = SEQ // d // w
        nb_shift = nb.bit_length() - 1

        def rows(start, d=d):
            return pl.ds(start, w) if d == 1 else pl.ds(start, w, stride=d)

        def block(i, carry, p=p, d=d, nb=nb, nb_shift=nb_shift, rows=rows):
            r = lax.shift_right_logical(i, nb_shift)
            n = jnp.bitwise_and(i, nb - 1)
            start = r + n * (w * d)
            qb = q_ref[rows(start), :].astype(BF16)
            s_c = _dot_nt(qb, k_ref[rows(start), :].astype(BF16)) * scale + bias_sc[p, :, w:2 * w]
            s_c = jnp.where(cur_ok, s_c, NEG_INF)
            m = jnp.max(s_c, axis=-1, keepdims=True)
            if nb > 1:
                pstart = r + jnp.maximum(n - 1, 0) * (w * d)
                s_p = _dot_nt(qb, k_ref[rows(pstart), :].astype(BF16)) * scale + bias_sc[p, :, 0:w]
                s_p = jnp.where((col >= row) & (n > 0), s_p, NEG_INF)
                m = jnp.maximum(m, jnp.max(s_p, axis=-1, keepdims=True))
            p_c = jnp.exp(s_c - m)
            den = jnp.sum(p_c, axis=-1, keepdims=True)
            acc = _dot(p_c.astype(BF16), v_ref[rows(start), :].astype(BF16))
            if nb > 1:
                p_p = jnp.exp(s_p - m)
                den = den + jnp.sum(p_p, axis=-1, keepdims=True)
                acc = acc + _dot(p_p.astype(BF16), v_ref[rows(pstart), :].astype(BF16))
            o_sc[p, rows(start), :] = acc / den
            l_sc[p, rows(start), :] = jnp.broadcast_to(m + jnp.log(den), (w, HEAD_DIM))
            return carry

        lax.fori_loop(0, d * nb, block, 0)

    chunk = 256

    def combine(c, carry):
        rs = pl.ds(pl.multiple_of(c * chunk, chunk), chunk)
        l1, l2, l3 = l_sc[0, rs, :], l_sc[1, rs, :], l_sc[2, rs, :]
        lm = jnp.maximum(jnp.maximum(l1, l2), l3)
        e1, e2, e3 = jnp.exp(l1 - lm), jnp.exp(l2 - lm), jnp.exp(l3 - lm)
        tot = e1 + e2 + e3
        out_ref[rs, :] = (e1 / tot) * o_sc[0, rs, :] + (e2 / tot) * o_sc[1, rs, :] + (e3 / tot) * o_sc[2, rs, :]
        return carry

    lax.fori_loop(0, SEQ // chunk, combine, 0)


def _attn(proj, bias_all):
    t = proj.shape[0]
    nblk = D_PROJ_MAIN // HEAD_DIM
    qi, ki, vi = nblk - 3 * A_HEADS, nblk - 2 * A_HEADS, nblk - A_HEADS
    npat = len(DILATED_PATTERNS)

    def head(base):
        return pl.BlockSpec((SEQ, HEAD_DIM), lambda b, h: (b, base + h))

    return pl.pallas_call(
        _attn_kernel,
        grid=(t // SEQ, A_HEADS),
        in_specs=[head(qi), head(ki), head(vi),
                  pl.BlockSpec((npat, None, 1, 2 * ATT_W), lambda b, h: (0, h, 0, 0))],
        out_specs=pl.BlockSpec((SEQ, HEAD_DIM), lambda b, h: (b, h)),
        out_shape=jax.ShapeDtypeStruct((t, D_A), F32),
        scratch_shapes=[pltpu.VMEM((npat, SEQ, HEAD_DIM), F32),
                        pltpu.VMEM((npat, SEQ, HEAD_DIM), F32),
                        pltpu.VMEM((npat, ATT_W, 2 * ATT_W), F32)],
        compiler_params=_cparams(("arbitrary", "arbitrary")),
        name="dilated_attn",
    )(proj, proj, proj, bias_all)


def _t5_causal_bucket(dist):
    max_exact = REL_BUCKETS // 2
    d = np.maximum(dist, 1).astype(np.float32)
    large = max_exact + (np.log(d / max_exact) / np.log(REL_MAX_DIST / max_exact)
                         * (REL_BUCKETS - max_exact)).astype(np.int32)
    return np.where(dist < max_exact, dist, np.minimum(large, REL_BUCKETS - 1)).astype(np.int32)


def _attn_bias(rel_bias):
    w = ATT_W
    j = np.clip(w - np.arange(2 * w), 0, w)
    tabs = [rel_bias[_t5_causal_bucket(j * dil)] for _, dil in DILATED_PATTERNS]
    return jnp.transpose(jnp.stack(tabs), (0, 2, 1))[:, :, None, :].astype(F32)


def _head_rms(x, g_ref, base):
    outs = []
    for h in range(x.shape[1] // HEAD_DIM):
        sl = slice(h * HEAD_DIM, (h + 1) * HEAD_DIM)
        outs.append(_rms(x[:, sl], g_ref[:, base + h * HEAD_DIM:base + (h + 1) * HEAD_DIM]))
    return outs


def _outproj_kernel(hm_ref, om_ref, ha_ref, x_ref, mod_ref, hg_ref, g_ref, w_ref, out_ref, cat_sc):
    gate = jax.nn.sigmoid(om_ref[...])
    for h, seg in enumerate(_head_rms(hm_ref[...], hg_ref, 0)):
        sl = slice(h * HEAD_DIM, (h + 1) * HEAD_DIM)
        cat_sc[:, sl] = (seg * gate[:, sl]).astype(BF16)
    for h, seg in enumerate(_head_rms(ha_ref[...], hg_ref, D_M)):
        cat_sc[:, D_M + h * HEAD_DIM:D_M + (h + 1) * HEAD_DIM] = seg.astype(BF16)
    y = _dot(cat_sc[...], w_ref[...])
    out_ref[...] = x_ref[...] + mod_ref[2:3, :] * _rms(y, g_ref[...])


def _outproj(hm, proj, ha, x2, mod3, head_gains, g1, w_out_bf):
    tm = 256
    t = x2.shape[0]
    half = lambda blk: pl.BlockSpec((tm, D_M), lambda i: (i, blk))
    full = pl.BlockSpec((tm, D_MODEL), lambda i: (i, 0))
    vec = pl.BlockSpec((1, D_MODEL), lambda i: (0, 0))
    return pl.pallas_call(
        _outproj_kernel,
        grid=(t // tm,),
        in_specs=[half(0), half(3), half(0), full,
                  pl.BlockSpec((None, 6, D_MODEL), lambda i: (i // (SEQ // tm), 0, 0)),
                  vec, vec,
                  pl.BlockSpec((D_MODEL, D_MODEL), lambda i: (0, 0))],
        out_specs=full,
        out_shape=jax.ShapeDtypeStruct((t, D_MODEL), F32),
        scratch_shapes=[pltpu.VMEM((tm, D_MODEL), BF16)],
        compiler_params=_cparams(("arbitrary",)),
        name="out_proj",
    )(hm, proj, ha, x2, mod3, head_gains, g1, w_out_bf)


def _router_kernel(x_ref, mod_ref, g_ref, w_ref, rb_ref, h_ref, idx_ref, gate_ref):
    tm = x_ref.shape[0]
    h = _rms(x_ref[...], g_ref[...]) * (1.0 + mod_ref[4:5, :]) + mod_ref[3:4, :]
    h_ref[...] = h
    scores = jax.nn.sigmoid(_dot(h.astype(BF16), w_ref[...].astype(BF16))).T
    sel = scores + rb_ref[...]
    ge = N_EXPERTS // N_GROUPS
    grow = lax.broadcasted_iota(jnp.int32, (N_GROUPS, tm), 0)
    gscore = jnp.zeros((N_GROUPS, tm), F32)
    for g in range(N_GROUPS):
        blk = sel[g * ge:(g + 1) * ge, :]
        m1 = jnp.max(blk, axis=0, keepdims=True)
        is1 = blk == m1
        cnt = jnp.sum(is1.astype(F32), axis=0, keepdims=True)
        m2 = jnp.max(jnp.where(is1, NEG_INF, blk), axis=0, keepdims=True)
        gscore = jnp.where(grow == g, m1 + jnp.where(cnt >= 2.0, m1, m2), gscore)
    rank = jnp.zeros((N_GROUPS, tm), F32)
    for g in range(N_GROUPS):
        other = gscore[g:g + 1, :]
        beats = (other > gscore) | ((other == gscore) & (grow > g))
        rank = rank + beats.astype(F32)
    keep = rank < float(TOPK_GROUPS)
    xm = jnp.concatenate([jnp.where(keep[g:g + 1, :], sel[g * ge:(g + 1) * ge, :], NEG_INF)
                          for g in range(N_GROUPS)], axis=0)
    erow = lax.broadcasted_iota(jnp.int32, (N_EXPERTS, tm), 0).astype(F32)
    krow = lax.broadcasted_iota(jnp.int32, (TOP_K, tm), 0)
    idx = jnp.zeros((TOP_K, tm), F32)
    gk = jnp.zeros((TOP_K, tm), F32)
    for k in range(TOP_K):
        m = jnp.max(xm, axis=0, keepdims=True)
        first = jnp.min(jnp.where(xm == m, erow, float(N_EXPERTS)), axis=0, keepdims=True)
        hit = erow == first
        idx = jnp.where(krow == k, first, idx)
        gk = jnp.where(krow == k, jnp.sum(jnp.where(hit, scores, 0.0), axis=0, keepdims=True), gk)
        xm = jnp.where(hit, NEG_INF, xm)
    idx_ref[...] = idx.astype(jnp.int32)
    gate_ref[...] = gk / jnp.sum(gk, axis=0, keepdims=True) * ROUTED_SCALE


def _router(x1, mod3, g2, w_router, rb_col):
    tm = 256
    t = x1.shape[0]
    return pl.pallas_call(
        _router_kernel,
        grid=(t // tm,),
        in_specs=[pl.BlockSpec((tm, D_MODEL), lambda i: (i, 0)),
                  pl.BlockSpec((None, 6, D_MODEL), lambda i: (i // (SEQ // tm), 0, 0)),
                  pl.BlockSpec((1, D_MODEL), lambda i: (0, 0)),
                  pl.BlockSpec((D_MODEL, N_EXPERTS), lambda i: (0, 0)),
                  pl.BlockSpec((N_EXPERTS, 1), lambda i: (0, 0))],
        out_specs=[pl.BlockSpec((tm, D_MODEL), lambda i: (i, 0)),
                   pl.BlockSpec((TOP_K, tm), lambda i: (0, i)),
                   pl.BlockSpec((TOP_K, tm), lambda i: (0, i))],
        out_shape=[jax.ShapeDtypeStruct((t, D_MODEL), F32),
                   jax.ShapeDtypeStruct((TOP_K, t), jnp.int32),
                   jax.ShapeDtypeStruct((TOP_K, t), F32)],
        compiler_params=_cparams(("arbitrary",)),
        name="ffn_norm_router",
    )(x1, mod3, g2, w_router, rb_col)


def _experts_kernel(te_ref, first_ref, nxt_ref, par_ref, valid_ref, dst_ref, tok_ref, tokn_ref, wt_ref,
                    h_hbm, wg_hbm, wu_hbm, wd_hbm, y_hbm,
                    wg_f, wu_f, wd_f, wg_b, wu_b, wd_b, xbuf, ybuf, wsem, xsem, ysem):
    tm = MOE_TILE
    i = pl.program_id(0)
    last = pl.num_programs(0) - 1
    slot = par_ref[i]
    ys = i % 2

    def w_copies(e, s):
        return (pltpu.make_async_copy(wg_hbm.at[e], wg_f.at[s], wsem.at[s, 0]),
                pltpu.make_async_copy(wu_hbm.at[e], wu_f.at[s], wsem.at[s, 1]),
                pltpu.make_async_copy(wd_hbm.at[e], wd_f.at[s], wsem.at[s, 2]))

    def x_copy(r, s, row):
        return pltpu.make_async_copy(h_hbm.at[pl.ds(row, 1)], xbuf.at[s, pl.ds(r, 1)], xsem.at[s])

    def y_copy(r, s, row):
        return pltpu.make_async_copy(ybuf.at[s, pl.ds(r, 1)], y_hbm.at[pl.ds(row, 1)], ysem.at[s])

    def gather_rows(s, rows_ref):
        for r in range(tm):
            x_copy(r, s, rows_ref[0, r]).start()

    def wait_rows(s):
        for r in range(tm):
            y_copy(r, s, 0).wait()

    @pl.when(i == 0)
    def _():
        for cp in w_copies(te_ref[0], 0):
            cp.start(priority=1)

        @pl.when(valid_ref[0] == 1)
        def _():
            gather_rows(0, tok_ref)

        ybuf[1] = jnp.zeros((tm, D_MODEL), F32)
        for blk in range(2):
            cp = pltpu.make_async_copy(ybuf.at[1], y_hbm.at[pl.ds(y_hbm.shape[0] - (blk + 1) * tm, tm)], ysem.at[1])
            cp.start()
            cp.wait()

    @pl.when(first_ref[i] == 1)
    def _():
        for cp in w_copies(te_ref[i], slot):
            cp.wait()

        @pl.when(nxt_ref[i] >= 0)
        def _():
            for cp in w_copies(nxt_ref[i], 1 - slot):
                cp.start(priority=1)

        wg_b[...] = wg_f[slot].astype(BF16)
        wu_b[...] = wu_f[slot].astype(BF16)
        wd_b[...] = wd_f[slot].astype(BF16)

    @pl.when((i < last) & (valid_ref[jnp.minimum(i + 1, last)] == 1))
    def _():
        gather_rows(1 - ys, tokn_ref)

    @pl.when((i >= 2) & (valid_ref[jnp.maximum(i - 2, 0)] == 1))
    def _():
        wait_rows(ys)

    @pl.when(valid_ref[i] == 1)
    def _():
        for r in range(tm):
            x_copy(r, ys, 0).wait()
        x = xbuf[ys].astype(BF16)
        hid = (_silu(_dot(x, wg_b[...])) * _dot(x, wu_b[...])).astype(BF16)
        ybuf[ys] = _dot(hid, wd_b[...]) * wt_ref[...]
        for r in range(tm):
            y_copy(r, ys, dst_ref[0, r]).start(priority=r % 2)

    @pl.when(i == last)
    def _():
        @pl.when((i >= 1) & (valid_ref[jnp.maximum(i - 1, 0)] == 1))
        def _():
            wait_rows(1 - ys)

        @pl.when(valid_ref[i] == 1)
        def _():
            wait_rows(ys)


def _experts(sched, dst, tok, wts, h2, w_gate, w_up, w_down, n_rows_out):
    tm = MOE_TILE
    nt = dst.shape[0]
    any_spec = pl.BlockSpec(memory_space=pl.ANY)

    def rows_spec(ahead):
        return pl.BlockSpec((None, 1, tm), lambda i, *_: (jnp.minimum(i + ahead, nt - 1), 0, 0),
                            memory_space=pltpu.SMEM)

    grid_spec = pltpu.PrefetchScalarGridSpec(
        num_scalar_prefetch=len(sched),
        grid=(nt,),
        in_specs=[rows_spec(0), rows_spec(0), rows_spec(1),
                  pl.BlockSpec((tm, 1), lambda i, *_: (i, 0)),
                  any_spec, any_spec, any_spec, any_spec],
        out_specs=any_spec,
        scratch_shapes=[pltpu.VMEM((2, D_MODEL, D_EXPERT), F32),
                        pltpu.VMEM((2, D_MODEL, D_EXPERT), F32),
                        pltpu.VMEM((2, D_EXPERT, D_MODEL), F32),
                        pltpu.VMEM((D_MODEL, D_EXPERT), BF16),
                        pltpu.VMEM((D_MODEL, D_EXPERT), BF16),
                        pltpu.VMEM((D_EXPERT, D_MODEL), BF16),
                        pltpu.VMEM((2, tm, D_MODEL), F32),
                        pltpu.VMEM((2, tm, D_MODEL), F32),
                        pltpu.SemaphoreType.DMA((2, 3)),
                        pltpu.SemaphoreType.DMA((2,)),
                        pltpu.SemaphoreType.DMA((2,))],
    )
    return pl.pallas_call(
        _experts_kernel,
        grid_spec=grid_spec,
        out_shape=jax.ShapeDtypeStruct((n_rows_out, D_MODEL), F32),
        compiler_params=_cparams(("arbitrary",)),
        name="routed_experts",
    )(*sched, dst, tok, tok, wts, h2, w_gate, w_up, w_down)


def _final_kernel(x_ref, h_ref, *rest):
    y_refs = rest[:TOP_K]
    mod_ref, g_ref, wg_ref, wu_ref, wd_ref, out_ref = rest[TOP_K:]
    h = h_ref[...].astype(BF16)
    hid = (_silu(_dot(h, wg_ref[...])) * _dot(h, wu_ref[...])).astype(BF16)
    y = _dot(hid, wd_ref[...])
    for r in y_refs:
        y = y + r[...]
    out_ref[...] = x_ref[...] + mod_ref[5:6, :] * _rms(y, g_ref[...])


def _final(x1, h2, y_rows, mod3, g3, ws_gate, ws_up, ws_down):
    tm = 128
    t = x1.shape[0]
    full = pl.BlockSpec((tm, D_MODEL), lambda i: (i, 0))

    def slab(k):
        return pl.BlockSpec((tm, D_MODEL), lambda i: (k * (t // tm) + i, 0))

    return pl.pallas_call(
        _final_kernel,
        grid=(t // tm,),
        in_specs=[full, full] + [slab(k) for k in range(TOP_K)] + [
            pl.BlockSpec((None, 6, D_MODEL), lambda i: (i // (SEQ // tm), 0, 0)),
            pl.BlockSpec((1, D_MODEL), lambda i: (0, 0)),
            pl.BlockSpec((D_MODEL, D_EXPERT), lambda i: (0, 0)),
            pl.BlockSpec((D_MODEL, D_EXPERT), lambda i: (0, 0)),
            pl.BlockSpec((D_EXPERT, D_MODEL), lambda i: (0, 0))],
        out_specs=full,
        out_shape=jax.ShapeDtypeStruct((t, D_MODEL), F32),
        compiler_params=_cparams(("arbitrary",)),
        name="shared_expert_final",
    )(x1, h2, *([y_rows] * TOP_K), mod3, g3, ws_gate, ws_up, ws_down)


def _dispatch(idx_t, gate_t):
    t = idx_t.shape[1]
    a = t * TOP_K
    tm = MOE_TILE
    nt = (a + N_EXPERTS * (tm - 1) + tm - 1) // tm
    e_flat = idx_t.reshape(-1)
    order = jnp.argsort(e_flat).astype(jnp.int32)
    e_sorted = e_flat[order]
    experts = jnp.arange(N_EXPERTS, dtype=jnp.int32)
    starts = jnp.searchsorted(e_sorted, experts, side="left").astype(jnp.int32)
    counts = jnp.searchsorted(e_sorted, experts, side="right").astype(jnp.int32) - starts
    padded = (counts + tm - 1) // tm * tm
    pends = jnp.cumsum(padded)
    pstarts = pends - padded
    tile_start = jnp.arange(nt, dtype=jnp.int32) * tm
    te = jnp.clip(jnp.searchsorted(pends, tile_start, side="right"), 0, N_EXPERTS - 1).astype(jnp.int32)
    valid = tile_start < pends[-1]
    lane = jnp.arange(tm, dtype=jnp.int32)[None, :]
    j = tile_start[:, None] + lane - pstarts[te][:, None]
    row_ok = (j < counts[te][:, None]) & valid[:, None]
    asg = order[jnp.clip(starts[te][:, None] + j, 0, a - 1)]
    tiles = jnp.arange(nt, dtype=jnp.int32)
    dst = jnp.where(row_ok, asg, a + (tiles % 2)[:, None] * tm + lane).astype(jnp.int32)
    tok = jnp.where(row_ok, asg % t, 0).astype(jnp.int32)
    wt = jnp.where(row_ok, gate_t.reshape(-1)[asg], 0.0)
    first = jnp.concatenate([jnp.ones((1,), bool), te[1:] != te[:-1]])
    par = ((jnp.cumsum(first.astype(jnp.int32)) - 1) % 2).astype(jnp.int32)
    run_start = jnp.where(first, tiles, nt)
    next_start = jnp.concatenate([lax.cummin(run_start[::-1])[::-1][1:], jnp.full((1,), nt, jnp.int32)])
    nxt = jnp.where(next_start < nt, te[jnp.clip(next_start, 0, nt - 1)], -1).astype(jnp.int32)
    sched = (te, first.astype(jnp.int32), nxt, par, valid.astype(jnp.int32))
    return sched, dst[:, None, :], tok[:, None, :], wt.reshape(-1, 1)


def kernel(x, c, w_ada, b_ada, norm_gains, w_in, b_if, conv_w, conv_b, head_gains, rel_bias, w_out,
           w_router, router_bias, w_exp_gate, w_exp_up, w_exp_down, w_sh_gate, w_sh_up, w_sh_down):
    b, s, d = x.shape
    t = b * s
    x2 = x.reshape(t, d)
    for l in range(w_ada.shape[0]):
        c_pad = jnp.pad(c, ((0, 8 - b), (0, 0)))
        mod = _ada(c_pad, w_ada[l], b_ada[l][None, :])[:b]
        mod3 = mod.reshape(b, 6, d)
        g = norm_gains[l]
        w_gates = jnp.pad(w_in[l][:, D_PROJ_MAIN:], ((0, 0), (0, LANES - N_GATES)))
        bif = jnp.pad(b_if[l][None, :], ((0, 0), (0, LANES - N_GATES)))
        proj, gates = _inproj(x2, mod3, g[0][None, :], w_in[l], w_gates, bif)
        gates_t = gates[:, :N_GATES].reshape(b, s, N_GATES).transpose(0, 2, 1)
        hm = _mlstm(proj, gates, gates_t, conv_w[l], conv_b[l][None, :])
        ha = _attn(proj, _attn_bias(rel_bias))
        x1 = _outproj(hm, proj, ha, x2, mod3, head_gains[l][None, :], g[1][None, :], w_out[l].astype(BF16))
        h2, idx_t, gate_t = _router(x1, mod3, g[2][None, :], w_router[l], router_bias[l][:, None])
        sched, dst, tok, wts = _dispatch(idx_t, gate_t)
        y_rows = _experts(sched, dst, tok, wts, h2, w_exp_gate[l], w_exp_up[l], w_exp_down[l], t * TOP_K + 2 * MOE_TILE)
        x2 = _final(x1, h2, y_rows, mod3, g[3][None, :], w_sh_gate[l].astype(BF16), w_sh_up[l].astype(BF16),
                    w_sh_down[l].astype(BF16))
    return x2.reshape(b, s, d)
```

```python
import numpy as np
import jax
import jax.numpy as jnp
from jax import lax
from jax.experimental import pallas as pl
from jax.experimental.pallas import tpu as pltpu

D_MODEL = 2048
SEQ = 2048
HEAD_DIM = 128
M_HEADS = 8
A_HEADS = 8
D_M = M_HEADS * HEAD_DIM
D_A = A_HEADS * HEAD_DIM
CONV_K = 4
MLSTM_CHUNK = 128
DILATED_PATTERNS = ((128, 1), (512, 4), (2048, 16))
REL_BUCKETS = 32
REL_MAX_DIST = 2048
N_EXPERTS = 256
TOP_K = 8
N_GROUPS = 8
TOPK_GROUPS = 4
D_EXPERT = 512
ROUTED_SCALE = 2.5
NORM_EPS = 1e-6
D_PROJ_MAIN = 4 * D_M + 3 * D_A
N_GATES = 2 * M_HEADS
LANES = 128
ATT_W = 128
MOE_TILE = 128
VMEM_LIMIT = 56 * 1024 * 1024

F32 = jnp.float32
BF16 = jnp.bfloat16
NEG_INF = float("-inf")


def _cparams(sem):
    return pltpu.CompilerParams(dimension_semantics=sem, vmem_limit_bytes=VMEM_LIMIT)


def _dot(a, b):
    return jnp.dot(a, b, preferred_element_type=F32)


def _dot_nt(a, b):
    return lax.dot_general(a, b, (((1,), (1,)), ((), ())), preferred_element_type=F32)


def _dot_tn(a, b):
    return lax.dot_general(a, b, (((0,), (0,)), ((), ())), preferred_element_type=F32)


def _rms(x, g):
    ms = jnp.mean(x * x, axis=-1, keepdims=True)
    return x * lax.rsqrt(ms + NORM_EPS) * g


def _silu(x):
    return x * jax.nn.sigmoid(x)


def _ada_kernel(c_ref, w_ref, b_ref, o_ref):
    a = _silu(c_ref[...]).astype(BF16)
    o_ref[...] = _dot(a, w_ref[...].astype(BF16)) + b_ref[...]


def _ada(c_pad, w_ada, b_ada):
    tn = 1024
    n = w_ada.shape[1]
    return pl.pallas_call(
        _ada_kernel,
        grid=(n // tn,),
        in_specs=[pl.BlockSpec((8, D_MODEL), lambda j: (0, 0)),
                  pl.BlockSpec((D_MODEL, tn), lambda j: (0, j)),
                  pl.BlockSpec((1, tn), lambda j: (0, j))],
        out_specs=pl.BlockSpec((8, tn), lambda j: (0, j)),
        out_shape=jax.ShapeDtypeStruct((8, n), F32),
        compiler_params=_cparams(("arbitrary",)),
        name="ada_mod",
    )(c_pad, w_ada, b_ada)


def _inproj_kernel(x_ref, mod_ref, g_ref, w_ref, wg_ref, bif_ref, proj_ref, gates_ref, h_sc):
    @pl.when(pl.program_id(1) == 0)
    def _():
        h = _rms(x_ref[...], g_ref[...]) * (1.0 + mod_ref[1:2, :]) + mod_ref[0:1, :]
        hb = h.astype(BF16)
        h_sc[...] = hb
        g = _dot(hb, wg_ref[...].astype(BF16)) + bif_ref[...]
        lane = lax.broadcasted_iota(jnp.int32, g.shape, 1)
        logf = jnp.minimum(g, 0.0) - jnp.log1p(jnp.exp(-jnp.abs(g)))
        gates_ref[...] = jnp.where(lane < M_HEADS, g, logf)

    proj_ref[...] = _dot(h_sc[...], w_ref[...].astype(BF16))


def _inproj(x2, mod3, g0, w_in, w_gates, bif):
    tm, tn = 1024, 512
    t = x2.shape[0]
    return pl.pallas_call(
        _inproj_kernel,
        grid=(t // tm, D_PROJ_MAIN // tn),
        in_specs=[pl.BlockSpec((tm, D_MODEL), lambda i, j: (i, 0)),
                  pl.BlockSpec((None, 6, D_MODEL), lambda i, j: (i // (SEQ // tm), 0, 0)),
                  pl.BlockSpec((1, D_MODEL), lambda i, j: (0, 0)),
                  pl.BlockSpec((D_MODEL, tn), lambda i, j: (0, j)),
                  pl.BlockSpec((D_MODEL, LANES), lambda i, j: (0, 0)),
                  pl.BlockSpec((1, LANES), lambda i, j: (0, 0))],
        out_specs=[pl.BlockSpec((tm, tn), lambda i, j: (i, j)),
                   pl.BlockSpec((tm, LANES), lambda i, j: (i, 0))],
        out_shape=[jax.ShapeDtypeStruct((t, D_PROJ_MAIN), F32),
                   jax.ShapeDtypeStruct((t, LANES), F32)],
        scratch_shapes=[pltpu.VMEM((tm, D_MODEL), BF16)],
        compiler_params=_cparams(("arbitrary", "arbitrary")),
        name="in_proj",
    )(x2, mod3, g0, w_in, w_gates, bif)


def _mlstm_kernel(q_ref, k_ref, v_ref, g_ref, gt_ref, cwq_ref, cwk_ref, cbq_ref, cbk_ref, o_ref,
                  ct_sc, n_sc, m_sc, pq_sc, pk_sc):
    lc = MLSTM_CHUNK

    @pl.when(pl.program_id(1) == 0)
    def _():
        ct_sc[...] = jnp.zeros_like(ct_sc)
        n_sc[...] = jnp.zeros_like(n_sc)
        m_sc[...] = jnp.zeros_like(m_sc)
        pq_sc[...] = jnp.zeros_like(pq_sc)
        pk_sc[...] = jnp.zeros_like(pk_sc)

    row_w = lax.broadcasted_iota(jnp.int32, (lc, D_M), 0)

    def conv(cur, prev, w_ref, b_ref):
        acc = cur * w_ref[CONV_K - 1:CONV_K, :] + b_ref[...]
        for s in range(1, CONV_K):
            sh = jnp.where(row_w >= s, pltpu.roll(cur, s, 0), pltpu.roll(prev, s, 0))
            acc = acc + sh * w_ref[CONV_K - 1 - s:CONV_K - s, :]
        return _silu(acc)

    q_raw = q_ref[...]
    k_raw = k_ref[...]
    q_all = conv(q_raw, pq_sc[...], cwq_ref, cbq_ref)
    k_all = conv(k_raw, pk_sc[...], cwk_ref, cbk_ref) * (HEAD_DIM ** -0.5)
    pq_sc[...] = q_raw
    pk_sc[...] = k_raw

    row = lax.broadcasted_iota(jnp.int32, (lc, lc), 0)
    col = lax.broadcasted_iota(jnp.int32, (lc, lc), 1)
    causal = col <= row
    tril = causal.astype(F32)
    triu = (row <= col).astype(F32)
    g = g_ref[...]
    gt = gt_ref[...]
    b_cols = jnp.dot(tril, g, precision=lax.Precision.HIGHEST, preferred_element_type=F32)
    b_rows = jnp.dot(gt, triu, precision=lax.Precision.HIGHEST, preferred_element_type=F32)

    for h in range(M_HEADS):
        sl = slice(h * HEAD_DIM, (h + 1) * HEAD_DIM)
        i_col = g[:, h:h + 1]
        b_col = b_cols[:, M_HEADS + h:M_HEADS + h + 1]
        i_row = gt[h:h + 1, :]
        b_row = b_rows[M_HEADS + h:M_HEADS + h + 1, :]
        m_prev = m_sc[h:h + 1, 0:1]
        n_prev = n_sc[h:h + 1, :]
        ct_prev = ct_sc[h]
        qh = q_all[:, sl]
        kh = k_all[:, sl]
        vh = v_ref[:, sl]
        qb = qh.astype(BF16)
        kb = kh.astype(BF16)

        dmat = jnp.where(causal, b_col - b_row + i_row, NEG_INF)
        inter = b_col + m_prev
        m_t = jnp.maximum(inter, jnp.max(dmat, axis=-1, keepdims=True))
        w_intra = jnp.exp(dmat - m_t)
        w_inter = jnp.exp(inter - m_t)
        s = _dot_nt(qb, kb) * w_intra
        num = _dot(s.astype(BF16), vh.astype(BF16)) + w_inter * _dot(qb, ct_prev.astype(BF16))
        den = jnp.sum(s, axis=-1, keepdims=True) + w_inter * jnp.sum(qh * n_prev, axis=-1, keepdims=True)
        o_ref[:, sl] = num / jnp.maximum(jnp.abs(den), jnp.exp(-m_t))

        b_last = b_col[lc - 1:lc, :]
        log_w = b_last - b_col + i_col
        m_new = jnp.maximum(b_last + m_prev, jnp.max(log_w, axis=0, keepdims=True))
        w_upd = jnp.exp(log_w - m_new)
        decay = jnp.exp(b_last + m_prev - m_new)
        ct_sc[h] = decay * ct_prev + _dot_tn(kb, (vh * w_upd).astype(BF16))
        n_sc[h:h + 1, :] = decay * n_prev + jnp.sum(kh * w_upd, axis=0, keepdims=True)
        m_sc[h:h + 1, :] = jnp.broadcast_to(m_new, (1, LANES))


def _mlstm(proj, gates, gates_t, conv_w, conv_b):
    lc = MLSTM_CHUNK
    nc = SEQ // lc
    t = proj.shape[0]
    return pl.pallas_call(
        _mlstm_kernel,
        grid=(t // SEQ, nc),
        in_specs=[pl.BlockSpec((lc, D_M), lambda b, c: (b * nc + c, 0)),
                  pl.BlockSpec((lc, D_M), lambda b, c: (b * nc + c, 1)),
                  pl.BlockSpec((lc, D_M), lambda b, c: (b * nc + c, 2)),
                  pl.BlockSpec((lc, LANES), lambda b, c: (b * nc + c, 0)),
                  pl.BlockSpec((None, N_GATES, lc), lambda b, c: (b, 0, c)),
                  pl.BlockSpec((CONV_K, D_M), lambda b, c: (0, 0)),
                  pl.BlockSpec((CONV_K, D_M), lambda b, c: (0, 1)),
                  pl.BlockSpec((1, D_M), lambda b, c: (0, 0)),
                  pl.BlockSpec((1, D_M), lambda b, c: (0, 1))],
        out_specs=pl.BlockSpec((lc, D_M), lambda b, c: (b * nc + c, 0)),
        out_shape=jax.ShapeDtypeStruct((t, D_M), F32),
        scratch_shapes=[pltpu.VMEM((M_HEADS, HEAD_DIM, HEAD_DIM), F32),
                        pltpu.VMEM((M_HEADS, HEAD_DIM), F32),
                        pltpu.VMEM((M_HEADS, LANES), F32),
                        pltpu.VMEM((lc, D_M), F32),
                        pltpu.VMEM((lc, D_M), F32)],
        compiler_params=_cparams(("arbitrary", "arbitrary")),
        name="mlstm",
    )(proj, proj, proj, gates, gates_t, conv_w, conv_w, conv_b, conv_b)


def _attn_kernel(q_ref, k_ref, v_ref, bias_ref, out_ref, o_sc, l_sc, bias_sc):
    w = ATT_W
    row = lax.broadcasted_iota(jnp.int32, (w, w), 0)
    col = lax.broadcasted_iota(jnp.int32, (w, w), 1)
    cur_ok = col <= row
    scale = HEAD_DIM ** -0.5
    for p in range(len(DILATED_PATTERNS)):
        bias_sc[p] = pltpu.roll(jnp.broadcast_to(bias_ref[p], (w, 2 * w)), 0, 1, stride=1, stride_axis=0)
    for p, (_, d) in enumerate(DILATED_PATTERNS):
        nb = SEQ // d // w
        nb_shift = nb.bit_length() - 1

        def rows(start, d=d):
            return pl.ds(start, w) if d == 1 else pl.ds(start, w, stride=d)

        def block(i, carry, p=p, d=d, nb=nb, nb_shift=nb_shift, rows=rows):
            r = lax.shift_right_logical(i, nb_shift)
            n = jnp.bitwise_and(i, nb - 1)
            start = r + n * (w * d)
            qb = q_ref[rows(start), :].astype(BF16)
            s_c = _dot_nt(qb, k_ref[rows(start), :].astype(BF16)) * scale + bias_sc[p, :, w:2 * w]
            s_c = jnp.where(cur_ok, s_c, NEG_INF)
            m = jnp.max(s_c, axis=-1, keepdims=True)
            if nb > 1:
                pstart = r + jnp.maximum(n - 1, 0) * (w * d)
                s_p = _dot_nt(qb, k_ref[rows(pstart), :].astype(BF16)) * scale + bias_sc[p, :, 0:w]
                s_p = jnp.where((col >= row) & (n > 0), s_p, NEG_INF)
                m = jnp.maximum(m, jnp.max(s_p, axis=-1, keepdims=True))
            p_c = jnp.exp(s_c - m)
            den = jnp.sum(p_c, axis=-1, keepdims=True)
            acc = _dot(p_c.astype(BF16), v_ref[rows(start), :].astype(BF16))
            if nb > 1:
                p_p = jnp.exp(s_p - m)
                den = den + jnp.sum(p_p, axis=-1, keepdims=True)
                acc = acc + _dot(p_p.astype(BF16), v_ref[rows(pstart), :].astype(BF16))
            o_sc[p, rows(start), :] = acc / den
            l_sc[p, rows(start), :] = jnp.broadcast_to(m + jnp.log(den), (w, HEAD_DIM))
            return carry

        lax.fori_loop(0, d * nb, block, 0)

    chunk = 256

    def combine(c, carry):
        rs = pl.ds(pl.multiple_of(c * chunk, chunk), chunk)
        l1, l2, l3 = l_sc[0, rs, :], l_sc[1, rs, :], l_sc[2, rs, :]
        lm = jnp.maximum(jnp.maximum(l1, l2), l3)
        e1, e2, e3 = jnp.exp(l1 - lm), jnp.exp(l2 - lm), jnp.exp(l3 - lm)
        tot = e1 + e2 + e3
        out_ref[rs, :] = (e1 / tot) * o_sc[0, rs, :] + (e2 / tot) * o_sc[1, rs, :] + (e3 / tot) * o_sc[2, rs, :]
        return carry

    lax.fori_loop(0, SEQ // chunk, combine, 0)


def _attn(proj, bias_all):
    t = proj.shape[0]
    nblk = D_PROJ_MAIN // HEAD_DIM
    qi, ki, vi = nblk - 3 * A_HEADS, nblk - 2 * A_HEADS, nblk - A_HEADS
    npat = len(DILATED_PATTERNS)

    def head(base):
        return pl.BlockSpec((SEQ, HEAD_DIM), lambda b, h: (b, base + h))

    return pl.pallas_call(
        _attn_kernel,
        grid=(t // SEQ, A_HEADS),
        in_specs=[head(qi), head(ki), head(vi),
                  pl.BlockSpec((npat, None, 1, 2 * ATT_W), lambda b, h: (0, h, 0, 0))],
        out_specs=pl.BlockSpec((SEQ, HEAD_DIM), lambda b, h: (b, h)),
        out_shape=jax.ShapeDtypeStruct((t, D_A), F32),
        scratch_shapes=[pltpu.VMEM((npat, SEQ, HEAD_DIM), F32),
                        pltpu.VMEM((npat, SEQ, HEAD_DIM), F32),
                        pltpu.VMEM((npat, ATT_W, 2 * ATT_W), F32)],
        compiler_params=_cparams(("arbitrary", "arbitrary")),
        name="dilated_attn",
    )(proj, proj, proj, bias_all)


def _t5_causal_bucket(dist):
    max_exact = REL_BUCKETS // 2
    d = np.maximum(dist, 1).astype(np.float32)
    large = max_exact + (np.log(d / max_exact) / np.log(REL_MAX_DIST / max_exact)
                         * (REL_BUCKETS - max_exact)).astype(np.int32)
    return np.where(dist < max_exact, dist, np.minimum(large, REL_BUCKETS - 1)).astype(np.int32)


def _attn_bias(rel_bias):
    w = ATT_W
    j = np.clip(w - np.arange(2 * w), 0, w)
    tabs = [rel_bias[_t5_causal_bucket(j * dil)] for _, dil in DILATED_PATTERNS]
    return jnp.transpose(jnp.stack(tabs), (0, 2, 1))[:, :, None, :].astype(F32)


def _head_rms(x, g_ref, base):
    outs = []
    for h in range(x.shape[1] // HEAD_DIM):
        sl = slice(h * HEAD_DIM, (h + 1) * HEAD_DIM)
        outs.append(_rms(x[:, sl], g_ref[:, base + h * HEAD_DIM:base + (h + 1) * HEAD_DIM]))
    return outs


def _outproj_kernel(hm_ref, om_ref, ha_ref, x_ref, mod_ref, hg_ref, g_ref, w_ref, out_ref, cat_sc):
    gate = jax.nn.sigmoid(om_ref[...])
    for h, seg in enumerate(_head_rms(hm_ref[...], hg_ref, 0)):
        sl = slice(h * HEAD_DIM, (h + 1) * HEAD_DIM)
        cat_sc[:, sl] = (seg * gate[:, sl]).astype(BF16)
    for h, seg in enumerate(_head_rms(ha_ref[...], hg_ref, D_M)):
        cat_sc[:, D_M + h * HEAD_DIM:D_M + (h + 1) * HEAD_DIM] = seg.astype(BF16)
    y = _dot(cat_sc[...], w_ref[...])
    out_ref[...] = x_ref[...] + mod_ref[2:3, :] * _rms(y, g_ref[...])


def _outproj(hm, proj, ha, x2, mod3, head_gains, g1, w_out_bf):
    tm = 256
    t = x2.shape[0]
    half = lambda blk: pl.BlockSpec((tm, D_M), lambda i: (i, blk))
    full = pl.BlockSpec((tm, D_MODEL), lambda i: (i, 0))
    vec = pl.BlockSpec((1, D_MODEL), lambda i: (0, 0))
    return pl.pallas_call(
        _outproj_kernel,
        grid=(t // tm,),
        in_specs=[half(0), half(3), half(0), full,
                  pl.BlockSpec((None, 6, D_MODEL), lambda i: (i // (SEQ // tm), 0, 0)),
                  vec, vec,
                  pl.BlockSpec((D_MODEL, D_MODEL), lambda i: (0, 0))],
        out_specs=full,
        out_shape=jax.ShapeDtypeStruct((t, D_MODEL), F32),
        scratch_shapes=[pltpu.VMEM((tm, D_MODEL), BF16)],
        compiler_params=_cparams(("arbitrary",)),
        name="out_proj",
    )(hm, proj, ha, x2, mod3, head_gains, g1, w_out_bf)


def _router_kernel(x_ref, mod_ref, g_ref, w_ref, rb_ref, h_ref, idx_ref, gate_ref, cnt_ref):
    tm = x_ref.shape[0]
    h = _rms(x_ref[...], g_ref[...]) * (1.0 + mod_ref[4:5, :]) + mod_ref[3:4, :]
    h_ref[...] = h
    scores = jax.nn.sigmoid(_dot(h.astype(BF16), w_ref[...].astype(BF16))).T
    sel = scores + rb_ref[...]
    ge = N_EXPERTS // N_GROUPS
    grow = lax.broadcasted_iota(jnp.int32, (N_GROUPS, tm), 0)
    gscore = jnp.zeros((N_GROUPS, tm), F32)
    for g in range(N_GROUPS):
        blk = sel[g * ge:(g + 1) * ge, :]
        m1 = jnp.max(blk, axis=0, keepdims=True)
        is1 = blk == m1
        cnt = jnp.sum(is1.astype(F32), axis=0, keepdims=True)
        m2 = jnp.max(jnp.where(is1, NEG_INF, blk), axis=0, keepdims=True)
        gscore = jnp.where(grow == g, m1 + jnp.where(cnt >= 2.0, m1, m2), gscore)
    rank = jnp.zeros((N_GROUPS, tm), F32)
    for g in range(N_GROUPS):
        other = gscore[g:g + 1, :]
        beats = (other > gscore) | ((other == gscore) & (grow > g))
        rank = rank + beats.astype(F32)
    keep = rank < float(TOPK_GROUPS)
    xm = jnp.concatenate([jnp.where(keep[g:g + 1, :], sel[g * ge:(g + 1) * ge, :], NEG_INF)
                          for g in range(N_GROUPS)], axis=0)
    erow = lax.broadcasted_iota(jnp.int32, (N_EXPERTS, tm), 0).astype(F32)
    krow = lax.broadcasted_iota(jnp.int32, (TOP_K, tm), 0)
    idx = jnp.zeros((TOP_K, tm), F32)
    gk = jnp.zeros((TOP_K, tm), F32)
    chosen = jnp.zeros((N_EXPERTS, tm), F32)
    for k in range(TOP_K):
        m = jnp.max(xm, axis=0, keepdims=True)
        first = jnp.min(jnp.where(xm == m, erow, float(N_EXPERTS)), axis=0, keepdims=True)
        hit = erow == first
        idx = jnp.where(krow == k, first, idx)
        gk = jnp.where(krow == k, jnp.sum(jnp.where(hit, scores, 0.0), axis=0, keepdims=True), gk)
        chosen = chosen + hit.astype(F32)
        xm = jnp.where(hit, NEG_INF, xm)
    idx_ref[...] = idx.astype(jnp.int32)
    gate_ref[...] = gk / jnp.sum(gk, axis=0, keepdims=True) * ROUTED_SCALE

    @pl.when(pl.program_id(0) == 0)
    def _():
        cnt_ref[...] = jnp.zeros_like(cnt_ref)

    cnt_ref[...] += jnp.sum(chosen, axis=1, keepdims=True)


def _router(x1, mod3, g2, w_router, rb_col):
    tm = 256
    t = x1.shape[0]
    return pl.pallas_call(
        _router_kernel,
        grid=(t // tm,),
        in_specs=[pl.BlockSpec((tm, D_MODEL), lambda i: (i, 0)),
                  pl.BlockSpec((None, 6, D_MODEL), lambda i: (i // (SEQ // tm), 0, 0)),
                  pl.BlockSpec((1, D_MODEL), lambda i: (0, 0)),
                  pl.BlockSpec((D_MODEL, N_EXPERTS), lambda i: (0, 0)),
                  pl.BlockSpec((N_EXPERTS, 1), lambda i: (0, 0))],
        out_specs=[pl.BlockSpec((tm, D_MODEL), lambda i: (i, 0)),
                   pl.BlockSpec((TOP_K, tm), lambda i: (0, i)),
                   pl.BlockSpec((TOP_K, tm), lambda i: (0, i)),
                   pl.BlockSpec((N_EXPERTS, 1), lambda i: (0, 0))],
        out_shape=[jax.ShapeDtypeStruct((t, D_MODEL), F32),
                   jax.ShapeDtypeStruct((TOP_K, t), jnp.int32),
                   jax.ShapeDtypeStruct((TOP_K, t), F32),
                   jax.ShapeDtypeStruct((N_EXPERTS, 1), F32)],
        compiler_params=_cparams(("arbitrary",)),
        name="ffn_norm_router",
    )(x1, mod3, g2, w_router, rb_col)


def _experts_kernel(te_ref, first_ref, nxt_ref, par_ref, valid_ref, dst_ref, tok0_ref, tok1_ref, tok2_ref, wt_ref,
                    h_hbm, wg_hbm, wu_hbm, wd_hbm, y_hbm,
                    wg_f, wu_f, wd_f, wg_b, wu_b, wd_b, xbuf, ybuf, wsem, xsem, ysem):
    tm = MOE_TILE
    i = pl.program_id(0)
    last = pl.num_programs(0) - 1
    slot = par_ref[i]
    ys = i % 2
    xs = lax.rem(i, 3)
    xs1 = lax.rem(i + 1, 3)
    xs2 = lax.rem(i + 2, 3)

    def w_copies(e, s):
        return (pltpu.make_async_copy(wg_hbm.at[e], wg_f.at[s], wsem.at[s, 0]),
                pltpu.make_async_copy(wu_hbm.at[e], wu_f.at[s], wsem.at[s, 1]),
                pltpu.make_async_copy(wd_hbm.at[e], wd_f.at[s], wsem.at[s, 2]))

    def x_copy(r, s, row):
        return pltpu.make_async_copy(h_hbm.at[pl.ds(row, 1)], xbuf.at[s, pl.ds(r, 1)], xsem.at[s])

    def y_copy(r, s, row):
        return pltpu.make_async_copy(ybuf.at[s, pl.ds(r, 1)], y_hbm.at[pl.ds(row, 1)], ysem.at[s])

    def gather_rows(s, rows_ref):
        for r in range(tm):
            x_copy(r, s, rows_ref[0, r]).start()

    def wait_rows(s):
        for r in range(tm):
            y_copy(r, s, 0).wait()

    def wait_gather(s):
        for r in range(tm):
            x_copy(r, s, 0).wait()

    @pl.when(i == 0)
    def _():
        for cp in w_copies(te_ref[0], 0):
            cp.start(priority=1)
        gather_rows(0, tok0_ref)
        gather_rows(1, tok1_ref)

        ybuf[1] = jnp.zeros((tm, D_MODEL), F32)
        for blk in range(2):
            cp = pltpu.make_async_copy(ybuf.at[1], y_hbm.at[pl.ds(y_hbm.shape[0] - (blk + 1) * tm, tm)], ysem.at[1])
            cp.start()
            cp.wait()

    @pl.when(first_ref[i] == 1)
    def _():
        for cp in w_copies(te_ref[i], slot):
            cp.wait()

        @pl.when(nxt_ref[i] >= 0)
        def _():
            for cp in w_copies(nxt_ref[i], 1 - slot):
                cp.start(priority=1)

        wg_b[...] = wg_f[slot].astype(BF16)
        wu_b[...] = wu_f[slot].astype(BF16)
        wd_b[...] = wd_f[slot].astype(BF16)

    issued_two_back = (i < 2) | (valid_ref[jnp.maximum(i - 2, 0)] == 1)

    @pl.when(issued_two_back)
    def _():
        wait_gather(xs)

    @pl.when((i >= 2) & (valid_ref[jnp.maximum(i - 2, 0)] == 1))
    def _():
        wait_rows(ys)

    @pl.when(valid_ref[i] == 1)
    def _():
        x = xbuf[xs].astype(BF16)
        hid = (_silu(_dot(x, wg_b[...])) * _dot(x, wu_b[...])).astype(BF16)
        ybuf[ys] = _dot(hid, wd_b[...]) * wt_ref[...]
        for r in range(tm):
            y_copy(r, ys, dst_ref[0, r]).start()
            x_copy(r, xs2, tok2_ref[0, r]).start()

    @pl.when(i == last)
    def _():
        @pl.when((i >= 1) & (valid_ref[jnp.maximum(i - 1, 0)] == 1))
        def _():
            wait_rows(1 - ys)
            wait_gather(xs1)

        @pl.when(valid_ref[i] == 1)
        def _():
            wait_rows(ys)
            wait_gather(xs2)


def _experts(sched, dst, tok, wts, h2, w_gate, w_up, w_down, n_rows_out):
    tm = MOE_TILE
    nt = dst.shape[0]
    any_spec = pl.BlockSpec(memory_space=pl.ANY)

    def rows_spec(ahead):
        return pl.BlockSpec((None, 1, tm), lambda i, *_: (i + ahead, 0, 0), memory_space=pltpu.SMEM)

    tok = jnp.pad(tok, ((0, 2), (0, 0), (0, 0)))
    grid_spec = pltpu.PrefetchScalarGridSpec(
        num_scalar_prefetch=len(sched),
        grid=(nt,),
        in_specs=[rows_spec(0), rows_spec(0), rows_spec(1), rows_spec(2),
                  pl.BlockSpec((tm, 1), lambda i, *_: (i, 0)),
                  any_spec, any_spec, any_spec, any_spec],
        out_specs=any_spec,
        scratch_shapes=[pltpu.VMEM((2, D_MODEL, D_EXPERT), F32),
                        pltpu.VMEM((2, D_MODEL, D_EXPERT), F32),
                        pltpu.VMEM((2, D_EXPERT, D_MODEL), F32),
                        pltpu.VMEM((D_MODEL, D_EXPERT), BF16),
                        pltpu.VMEM((D_MODEL, D_EXPERT), BF16),
                        pltpu.VMEM((D_EXPERT, D_MODEL), BF16),
                        pltpu.VMEM((3, tm, D_MODEL), F32),
                        pltpu.VMEM((2, tm, D_MODEL), F32),
                        pltpu.SemaphoreType.DMA((2, 3)),
                        pltpu.SemaphoreType.DMA((3,)),
                        pltpu.SemaphoreType.DMA((2,))],
    )
    return pl.pallas_call(
        _experts_kernel,
        grid_spec=grid_spec,
        out_shape=jax.ShapeDtypeStruct((n_rows_out, D_MODEL), F32),
        compiler_params=_cparams(("arbitrary",)),
        name="routed_experts",
    )(*sched, dst, tok, tok, tok, wts, h2, w_gate, w_up, w_down)


def _final_kernel(x_ref, h_ref, *rest):
    y_refs = rest[:TOP_K]
    mod_ref, g_ref, wg_ref, wu_ref, wd_ref, out_ref = rest[TOP_K:]
    h = h_ref[...].astype(BF16)
    hid = (_silu(_dot(h, wg_ref[...])) * _dot(h, wu_ref[...])).astype(BF16)
    y = _dot(hid, wd_ref[...])
    for r in y_refs:
        y = y + r[...]
    out_ref[...] = x_ref[...] + mod_ref[5:6, :] * _rms(y, g_ref[...])


def _final(x1, h2, y_rows, mod3, g3, ws_gate, ws_up, ws_down):
    tm = 128
    t = x1.shape[0]
    full = pl.BlockSpec((tm, D_MODEL), lambda i: (i, 0))

    def slab(k):
        return pl.BlockSpec((tm, D_MODEL), lambda i: (k * (t // tm) + i, 0))

    return pl.pallas_call(
        _final_kernel,
        grid=(t // tm,),
        in_specs=[full, full] + [slab(k) for k in range(TOP_K)] + [
            pl.BlockSpec((None, 6, D_MODEL), lambda i: (i // (SEQ // tm), 0, 0)),
            pl.BlockSpec((1, D_MODEL), lambda i: (0, 0)),
            pl.BlockSpec((D_MODEL, D_EXPERT), lambda i: (0, 0)),
            pl.BlockSpec((D_MODEL, D_EXPERT), lambda i: (0, 0)),
            pl.BlockSpec((D_EXPERT, D_MODEL), lambda i: (0, 0))],
        out_specs=full,
        out_shape=jax.ShapeDtypeStruct((t, D_MODEL), F32),
        compiler_params=_cparams(("arbitrary",)),
        name="shared_expert_final",
    )(x1, h2, *([y_rows] * TOP_K), mod3, g3, ws_gate, ws_up, ws_down)


def _dispatch(idx_t, gate_t, counts):
    t = idx_t.shape[1]
    a = t * TOP_K
    tm = MOE_TILE
    nt = (a + N_EXPERTS * (tm - 1) + tm - 1) // tm
    e_flat = idx_t.reshape(-1)
    order = jnp.argsort(e_flat).astype(jnp.int32)
    starts = jnp.cumsum(counts) - counts
    padded = (counts + tm - 1) // tm * tm
    pends = jnp.cumsum(padded)
    pstarts = pends - padded
    tile_start = jnp.arange(nt, dtype=jnp.int32) * tm
    te = jnp.minimum(jnp.sum(pends[None, :] <= tile_start[:, None], axis=1), N_EXPERTS - 1).astype(jnp.int32)
    valid = tile_start < pends[-1]
    lane = jnp.arange(tm, dtype=jnp.int32)[None, :]
    j = tile_start[:, None] + lane - pstarts[te][:, None]
    row_ok = (j < counts[te][:, None]) & valid[:, None]
    asg = order[jnp.clip(starts[te][:, None] + j, 0, a - 1)]
    tiles = jnp.arange(nt, dtype=jnp.int32)
    dst = jnp.where(row_ok, asg, a + (tiles % 2)[:, None] * tm + lane).astype(jnp.int32)
    tok = jnp.where(row_ok, asg % t, 0).astype(jnp.int32)
    wt = jnp.where(row_ok, gate_t.reshape(-1)[asg], 0.0)
    first = jnp.concatenate([jnp.ones((1,), bool), te[1:] != te[:-1]])
    par = ((jnp.cumsum(first.astype(jnp.int32)) - 1) % 2).astype(jnp.int32)
    run_start = jnp.where(first, tiles, nt)
    next_start = jnp.concatenate([lax.cummin(run_start[::-1])[::-1][1:], jnp.full((1,), nt, jnp.int32)])
    nxt = jnp.where(next_start < nt, te[jnp.clip(next_start, 0, nt - 1)], -1).astype(jnp.int32)
    sched = (te, first.astype(jnp.int32), nxt, par, valid.astype(jnp.int32))
    return sched, dst[:, None, :], tok[:, None, :], wt.reshape(-1, 1)


def kernel(x, c, w_ada, b_ada, norm_gains, w_in, b_if, conv_w, conv_b, head_gains, rel_bias, w_out,
           w_router, router_bias, w_exp_gate, w_exp_up, w_exp_down, w_sh_gate, w_sh_up, w_sh_down):
    b, s, d = x.shape
    t = b * s
    x2 = x.reshape(t, d)
    for l in range(w_ada.shape[0]):
        c_pad = jnp.pad(c, ((0, 8 - b), (0, 0)))
        mod = _ada(c_pad, w_ada[l], b_ada[l][None, :])[:b]
        mod3 = mod.reshape(b, 6, d)
        g = norm_gains[l]
        w_gates = jnp.pad(w_in[l][:, D_PROJ_MAIN:], ((0, 0), (0, LANES - N_GATES)))
        bif = jnp.pad(b_if[l][None, :], ((0, 0), (0, LANES - N_GATES)))
        proj, gates = _inproj(x2, mod3, g[0][None, :], w_in[l], w_gates, bif)
        gates_t = gates[:, :N_GATES].reshape(b, s, N_GATES).transpose(0, 2, 1)
        hm = _mlstm(proj, gates, gates_t, conv_w[l], conv_b[l][None, :])
        ha = _attn(proj, _attn_bias(rel_bias))
        x1 = _outproj(hm, proj, ha, x2, mod3, head_gains[l][None, :], g[1][None, :], w_out[l].astype(BF16))
        h2, idx_t, gate_t, cnt = _router(x1, mod3, g[2][None, :], w_router[l], router_bias[l][:, None])
        sched, dst, tok, wts = _dispatch(idx_t, gate_t, cnt[:, 0].astype(jnp.int32))
        y_rows = _experts(sched, dst, tok, wts, h2, w_exp_gate[l], w_exp_up[l], w_exp_down[l], t * TOP_K + 2 * MOE_TILE)
        x2 = _final(x1, h2, y_rows, mod3, g[3][None, :], w_sh_gate[l].astype(BF16), w_sh_up[l].astype(BF16),
                    w_sh_down[l].astype(BF16))
    return x2.reshape(b, s, d)
```

```python
import numpy as np
import jax
import jax.numpy as jnp
from jax import lax
from jax.experimental import pallas as pl
from jax.experimental.pallas import tpu as pltpu

D_MODEL = 2048
SEQ = 2048
HEAD_DIM = 128
M_HEADS = 8
A_HEADS = 8
D_M = M_HEADS * HEAD_DIM
D_A = A_HEADS * HEAD_DIM
CONV_K = 4
MLSTM_CHUNK = 128
DILATED_PATTERNS = ((128, 1), (512, 4), (2048, 16))
REL_BUCKETS = 32
REL_MAX_DIST = 2048
N_EXPERTS = 256
TOP_K = 8
N_GROUPS = 8
TOPK_GROUPS = 4
D_EXPERT = 512
ROUTED_SCALE = 2.5
NORM_EPS = 1e-6
D_PROJ_MAIN = 4 * D_M + 3 * D_A
N_GATES = 2 * M_HEADS
LANES = 128
ATT_W = 128
MOE_TILE = 128
VMEM_LIMIT = 56 * 1024 * 1024
EXPERTS_VMEM_LIMIT = 60 * 1024 * 1024

F32 = jnp.float32
BF16 = jnp.bfloat16
NEG_INF = float("-inf")


def _cparams(sem):
    return pltpu.CompilerParams(dimension_semantics=sem, vmem_limit_bytes=VMEM_LIMIT)


def _dot(a, b):
    return jnp.dot(a, b, preferred_element_type=F32)


def _dot_nt(a, b):
    return lax.dot_general(a, b, (((1,), (1,)), ((), ())), preferred_element_type=F32)


def _dot_tn(a, b):
    return lax.dot_general(a, b, (((0,), (0,)), ((), ())), preferred_element_type=F32)


def _rms(x, g):
    ms = jnp.mean(x * x, axis=-1, keepdims=True)
    return x * lax.rsqrt(ms + NORM_EPS) * g


def _silu(x):
    return x * jax.nn.sigmoid(x)


def _ada_kernel(c_ref, w_ref, b_ref, o_ref):
    a = _silu(c_ref[...]).astype(BF16)
    o_ref[...] = _dot(a, w_ref[...].astype(BF16)) + b_ref[...]


def _ada(c_pad, w_ada, b_ada):
    tn = 1024
    n = w_ada.shape[1]
    return pl.pallas_call(
        _ada_kernel,
        grid=(n // tn,),
        in_specs=[pl.BlockSpec((8, D_MODEL), lambda j: (0, 0)),
                  pl.BlockSpec((D_MODEL, tn), lambda j: (0, j)),
                  pl.BlockSpec((1, tn), lambda j: (0, j))],
        out_specs=pl.BlockSpec((8, tn), lambda j: (0, j)),
        out_shape=jax.ShapeDtypeStruct((8, n), F32),
        compiler_params=_cparams(("arbitrary",)),
        name="ada_mod",
    )(c_pad, w_ada, b_ada)


def _inproj_kernel(x_ref, mod_ref, g_ref, w_ref, wg_ref, bif_ref, proj_ref, gates_ref, h_sc):
    @pl.when(pl.program_id(1) == 0)
    def _():
        h = _rms(x_ref[...], g_ref[...]) * (1.0 + mod_ref[1:2, :]) + mod_ref[0:1, :]
        hb = h.astype(BF16)
        h_sc[...] = hb
        g = _dot(hb, wg_ref[...].astype(BF16)) + bif_ref[...]
        lane = lax.broadcasted_iota(jnp.int32, g.shape, 1)
        logf = jnp.minimum(g, 0.0) - jnp.log1p(jnp.exp(-jnp.abs(g)))
        gates_ref[...] = jnp.where(lane < M_HEADS, g, logf)

    proj_ref[...] = _dot(h_sc[...], w_ref[...].astype(BF16))


def _inproj(x2, mod3, g0, w_in, w_gates, bif):
    tm, tn = 1024, 512
    t = x2.shape[0]
    return pl.pallas_call(
        _inproj_kernel,
        grid=(t // tm, D_PROJ_MAIN // tn),
        in_specs=[pl.BlockSpec((tm, D_MODEL), lambda i, j: (i, 0)),
                  pl.BlockSpec((None, 6, D_MODEL), lambda i, j: (i // (SEQ // tm), 0, 0)),
                  pl.BlockSpec((1, D_MODEL), lambda i, j: (0, 0)),
                  pl.BlockSpec((D_MODEL, tn), lambda i, j: (0, j)),
                  pl.BlockSpec((D_MODEL, LANES), lambda i, j: (0, 0)),
                  pl.BlockSpec((1, LANES), lambda i, j: (0, 0))],
        out_specs=[pl.BlockSpec((tm, tn), lambda i, j: (i, j)),
                   pl.BlockSpec((tm, LANES), lambda i, j: (i, 0))],
        out_shape=[jax.ShapeDtypeStruct((t, D_PROJ_MAIN), F32),
                   jax.ShapeDtypeStruct((t, LANES), F32)],
        scratch_shapes=[pltpu.VMEM((tm, D_MODEL), BF16)],
        compiler_params=_cparams(("arbitrary", "arbitrary")),
        name="in_proj",
    )(x2, mod3, g0, w_in, w_gates, bif)


def _mlstm_kernel(q_ref, k_ref, v_ref, g_ref, gt_ref, cwq_ref, cwk_ref, cbq_ref, cbk_ref, o_ref,
                  ct_sc, n_sc, m_sc, pq_sc, pk_sc):
    lc = MLSTM_CHUNK

    @pl.when(pl.program_id(1) == 0)
    def _():
        ct_sc[...] = jnp.zeros_like(ct_sc)
        n_sc[...] = jnp.zeros_like(n_sc)
        m_sc[...] = jnp.zeros_like(m_sc)
        pq_sc[...] = jnp.zeros_like(pq_sc)
        pk_sc[...] = jnp.zeros_like(pk_sc)

    row_w = lax.broadcasted_iota(jnp.int32, (lc, D_M), 0)

    def conv(cur, prev, w_ref, b_ref):
        acc = cur * w_ref[CONV_K - 1:CONV_K, :] + b_ref[...]
        for s in range(1, CONV_K):
            sh = jnp.where(row_w >= s, pltpu.roll(cur, s, 0), pltpu.roll(prev, s, 0))
            acc = acc + sh * w_ref[CONV_K - 1 - s:CONV_K - s, :]
        return _silu(acc)

    q_raw = q_ref[...]
    k_raw = k_ref[...]
    q_all = conv(q_raw, pq_sc[...], cwq_ref, cbq_ref)
    k_all = conv(k_raw, pk_sc[...], cwk_ref, cbk_ref) * (HEAD_DIM ** -0.5)
    pq_sc[...] = q_raw
    pk_sc[...] = k_raw

    row = lax.broadcasted_iota(jnp.int32, (lc, lc), 0)
    col = lax.broadcasted_iota(jnp.int32, (lc, lc), 1)
    causal = col <= row
    tril = causal.astype(F32)
    triu = (row <= col).astype(F32)
    g = g_ref[...]
    gt = gt_ref[...]
    b_cols = jnp.dot(tril, g, precision=lax.Precision.HIGHEST, preferred_element_type=F32)
    b_rows = jnp.dot(gt, triu, precision=lax.Precision.HIGHEST, preferred_element_type=F32)

    for h in range(M_HEADS):
        sl = slice(h * HEAD_DIM, (h + 1) * HEAD_DIM)
        i_col = g[:, h:h + 1]
        b_col = b_cols[:, M_HEADS + h:M_HEADS + h + 1]
        i_row = gt[h:h + 1, :]
        b_row = b_rows[M_HEADS + h:M_HEADS + h + 1, :]
        m_prev = m_sc[h:h + 1, 0:1]
        n_prev = n_sc[h:h + 1, :]
        ct_prev = ct_sc[h]
        qh = q_all[:, sl]
        kh = k_all[:, sl]
        vh = v_ref[:, sl]
        qb = qh.astype(BF16)
        kb = kh.astype(BF16)

        dmat = jnp.where(causal, b_col - b_row + i_row, NEG_INF)
        inter = b_col + m_prev
        m_t = jnp.maximum(inter, jnp.max(dmat, axis=-1, keepdims=True))
        w_intra = jnp.exp(dmat - m_t)
        w_inter = jnp.exp(inter - m_t)
        s = _dot_nt(qb, kb) * w_intra
        num = _dot(s.astype(BF16), vh.astype(BF16)) + w_inter * _dot(qb, ct_prev.astype(BF16))
        den = jnp.sum(s, axis=-1, keepdims=True) + w_inter * jnp.sum(qh * n_prev, axis=-1, keepdims=True)
        o_ref[:, sl] = num / jnp.maximum(jnp.abs(den), jnp.exp(-m_t))

        b_last = b_col[lc - 1:lc, :]
        log_w = b_last - b_col + i_col
        m_new = jnp.maximum(b_last + m_prev, jnp.max(log_w, axis=0, keepdims=True))
        w_upd = jnp.exp(log_w - m_new)
        decay = jnp.exp(b_last + m_prev - m_new)
        ct_sc[h] = decay * ct_prev + _dot_tn(kb, (vh * w_upd).astype(BF16))
        n_sc[h:h + 1, :] = decay * n_prev + jnp.sum(kh * w_upd, axis=0, keepdims=True)
        m_sc[h:h + 1, :] = jnp.broadcast_to(m_new, (1, LANES))


def _mlstm(proj, gates, gates_t, conv_w, conv_b):
    lc = MLSTM_CHUNK
    nc = SEQ // lc
    t = proj.shape[0]
    return pl.pallas_call(
        _mlstm_kernel,
        grid=(t // SEQ, nc),
        in_specs=[pl.BlockSpec((lc, D_M), lambda b, c: (b * nc + c, 0)),
                  pl.BlockSpec((lc, D_M), lambda b, c: (b * nc + c, 1)),
                  pl.BlockSpec((lc, D_M), lambda b, c: (b * nc + c, 2)),
                  pl.BlockSpec((lc, LANES), lambda b, c: (b * nc + c, 0)),
                  pl.BlockSpec((None, N_GATES, lc), lambda b, c: (b, 0, c)),
                  pl.BlockSpec((CONV_K, D_M), lambda b, c: (0, 0)),
                  pl.BlockSpec((CONV_K, D_M), lambda b, c: (0, 1)),
                  pl.BlockSpec((1, D_M), lambda b, c: (0, 0)),
                  pl.BlockSpec((1, D_M), lambda b, c: (0, 1))],
        out_specs=pl.BlockSpec((lc, D_M), lambda b, c: (b * nc + c, 0)),
        out_shape=jax.ShapeDtypeStruct((t, D_M), F32),
        scratch_shapes=[pltpu.VMEM((M_HEADS, HEAD_DIM, HEAD_DIM), F32),
                        pltpu.VMEM((M_HEADS, HEAD_DIM), F32),
                        pltpu.VMEM((M_HEADS, LANES), F32),
                        pltpu.VMEM((lc, D_M), F32),
                        pltpu.VMEM((lc, D_M), F32)],
        compiler_params=_cparams(("arbitrary", "arbitrary")),
        name="mlstm",
    )(proj, proj, proj, gates, gates_t, conv_w, conv_w, conv_b, conv_b)


def _attn_kernel(q_ref, k_ref, v_ref, bias_ref, out_ref, o_sc, l_sc, bias_sc):
    w = ATT_W
    row = lax.broadcasted_iota(jnp.int32, (w, w), 0)
    col = lax.broadcasted_iota(jnp.int32, (w, w), 1)
    cur_ok = col <= row
    scale = HEAD_DIM ** -0.5
    for p in range(len(DILATED_PATTERNS)):
        bias_sc[p] = pltpu.roll(jnp.broadcast_to(bias_ref[p], (w, 2 * w)), 0, 1, stride=1, stride_axis=0)
    for p, (_, d) in enumerate(DILATED_PATTERNS):
        nb = SEQ // d // w
        nb_shift = nb.bit_length() - 1

        def rows(start, d=d):
            return pl.ds(start, w) if d == 1 else pl.ds(start, w, stride=d)

        def block(i, carry, p=p, d=d, nb=nb, nb_shift=nb_shift, rows=rows):
            r = lax.shift_right_logical(i, nb_shift)
            n = jnp.bitwise_and(i, nb - 1)
            start = r + n * (w * d)
            qb = q_ref[rows(start), :].astype(BF16)
            s_c = _dot_nt(qb, k_ref[rows(start), :].astype(BF16)) * scale + bias_sc[p, :, w:2 * w]
            s_c = jnp.where(cur_ok, s_c, NEG_INF)
            m = jnp.max(s_c, axis=-1, keepdims=True)
            if nb > 1:
                pstart = r + jnp.maximum(n - 1, 0) * (w * d)
                s_p = _dot_nt(qb, k_ref[rows(pstart), :].astype(BF16)) * scale + bias_sc[p, :, 0:w]
                s_p = jnp.where((col >= row) & (n > 0), s_p, NEG_INF)
                m = jnp.maximum(m, jnp.max(s_p, axis=-1, keepdims=True))
            p_c = jnp.exp(s_c - m)
            den = jnp.sum(p_c, axis=-1, keepdims=True)
            acc = _dot(p_c.astype(BF16), v_ref[rows(start), :].astype(BF16))
            if nb > 1:
                p_p = jnp.exp(s_p - m)
                den = den + jnp.sum(p_p, axis=-1, keepdims=True)
                acc = acc + _dot(p_p.astype(BF16), v_ref[rows(pstart), :].astype(BF16))
            o_sc[p, rows(start), :] = acc / den
            l_sc[p, rows(start), :] = jnp.broadcast_to(m + jnp.log(den), (w, HEAD_DIM))
            return carry

        lax.fori_loop(0, d * nb, block, 0)

    chunk = 256

    def combine(c, carry):
        rs = pl.ds(pl.multiple_of(c * chunk, chunk), chunk)
        l1, l2, l3 = l_sc[0, rs, :], l_sc[1, rs, :], l_sc[2, rs, :]
        lm = jnp.maximum(jnp.maximum(l1, l2), l3)
        e1, e2, e3 = jnp.exp(l1 - lm), jnp.exp(l2 - lm), jnp.exp(l3 - lm)
        tot = e1 + e2 + e3
        out_ref[rs, :] = (e1 / tot) * o_sc[0, rs, :] + (e2 / tot) * o_sc[1, rs, :] + (e3 / tot) * o_sc[2, rs, :]
        return carry

    lax.fori_loop(0, SEQ // chunk, combine, 0)


def _attn(proj, bias_all):
    t = proj.shape[0]
    nblk = D_PROJ_MAIN // HEAD_DIM
    qi, ki, vi = nblk - 3 * A_HEADS, nblk - 2 * A_HEADS, nblk - A_HEADS
    npat = len(DILATED_PATTERNS)

    def head(base):
        return pl.BlockSpec((SEQ, HEAD_DIM), lambda b, h: (b, base + h))

    return pl.pallas_call(
        _attn_kernel,
        grid=(t // SEQ, A_HEADS),
        in_specs=[head(qi), head(ki), head(vi),
                  pl.BlockSpec((npat, None, 1, 2 * ATT_W), lambda b, h: (0, h, 0, 0))],
        out_specs=pl.BlockSpec((SEQ, HEAD_DIM), lambda b, h: (b, h)),
        out_shape=jax.ShapeDtypeStruct((t, D_A), F32),
        scratch_shapes=[pltpu.VMEM((npat, SEQ, HEAD_DIM), F32),
                        pltpu.VMEM((npat, SEQ, HEAD_DIM), F32),
                        pltpu.VMEM((npat, ATT_W, 2 * ATT_W), F32)],
        compiler_params=_cparams(("arbitrary", "arbitrary")),
        name="dilated_attn",
    )(proj, proj, proj, bias_all)


def _t5_causal_bucket(dist):
    max_exact = REL_BUCKETS // 2
    d = np.maximum(dist, 1).astype(np.float32)
    large = max_exact + (np.log(d / max_exact) / np.log(REL_MAX_DIST / max_exact)
                         * (REL_BUCKETS - max_exact)).astype(np.int32)
    return np.where(dist < max_exact, dist, np.minimum(large, REL_BUCKETS - 1)).astype(np.int32)


def _attn_bias(rel_bias):
    w = ATT_W
    j = np.clip(w - np.arange(2 * w), 0, w)
    tabs = [rel_bias[_t5_causal_bucket(j * dil)] for _, dil in DILATED_PATTERNS]
    return jnp.transpose(jnp.stack(tabs), (0, 2, 1))[:, :, None, :].astype(F32)


def _head_rms(x, g_ref, base):
    outs = []
    for h in range(x.shape[1] // HEAD_DIM):
        sl = slice(h * HEAD_DIM, (h + 1) * HEAD_DIM)
        outs.append(_rms(x[:, sl], g_ref[:, base + h * HEAD_DIM:base + (h + 1) * HEAD_DIM]))
    return outs


def _outproj_kernel(hm_ref, om_ref, ha_ref, x_ref, mod_ref, hg_ref, g_ref, w_ref, out_ref, cat_sc):
    gate = jax.nn.sigmoid(om_ref[...])
    for h, seg in enumerate(_head_rms(hm_ref[...], hg_ref, 0)):
        sl = slice(h * HEAD_DIM, (h + 1) * HEAD_DIM)
        cat_sc[:, sl] = (seg * gate[:, sl]).astype(BF16)
    for h, seg in enumerate(_head_rms(ha_ref[...], hg_ref, D_M)):
        cat_sc[:, D_M + h * HEAD_DIM:D_M + (h + 1) * HEAD_DIM] = seg.astype(BF16)
    y = _dot(cat_sc[...], w_ref[...])
    out_ref[...] = x_ref[...] + mod_ref[2:3, :] * _rms(y, g_ref[...])


def _outproj(hm, proj, ha, x2, mod3, head_gains, g1, w_out_bf):
    tm = 256
    t = x2.shape[0]
    half = lambda blk: pl.BlockSpec((tm, D_M), lambda i: (i, blk))
    full = pl.BlockSpec((tm, D_MODEL), lambda i: (i, 0))
    vec = pl.BlockSpec((1, D_MODEL), lambda i: (0, 0))
    return pl.pallas_call(
        _outproj_kernel,
        grid=(t // tm,),
        in_specs=[half(0), half(3), half(0), full,
                  pl.BlockSpec((None, 6, D_MODEL), lambda i: (i // (SEQ // tm), 0, 0)),
                  vec, vec,
                  pl.BlockSpec((D_MODEL, D_MODEL), lambda i: (0, 0))],
        out_specs=full,
        out_shape=jax.ShapeDtypeStruct((t, D_MODEL), F32),
        scratch_shapes=[pltpu.VMEM((tm, D_MODEL), BF16)],
        compiler_params=_cparams(("arbitrary",)),
        name="out_proj",
    )(hm, proj, ha, x2, mod3, head_gains, g1, w_out_bf)


def _router_kernel(x_ref, mod_ref, g_ref, w_ref, rb_ref, h_ref, hp_ref, idx_ref, gate_ref, cnt_ref):
    tm = x_ref.shape[0]
    h = _rms(x_ref[...], g_ref[...]) * (1.0 + mod_ref[4:5, :]) + mod_ref[3:4, :]
    h_ref[...] = h
    hb = h.astype(BF16)
    bits = lax.bitcast_convert_type(hb.astype(F32), jnp.uint32)
    hp_ref[...] = bits[:, D_MODEL // 2:] | lax.shift_right_logical(bits[:, :D_MODEL // 2], jnp.uint32(16))
    scores = jax.nn.sigmoid(_dot(hb, w_ref[...].astype(BF16))).T
    sel = scores + rb_ref[...]
    ge = N_EXPERTS // N_GROUPS
    grow = lax.broadcasted_iota(jnp.int32, (N_GROUPS, tm), 0)
    gscore = jnp.zeros((N_GROUPS, tm), F32)
    for g in range(N_GROUPS):
        blk = sel[g * ge:(g + 1) * ge, :]
        m1 = jnp.max(blk, axis=0, keepdims=True)
        is1 = blk == m1
        cnt = jnp.sum(is1.astype(F32), axis=0, keepdims=True)
        m2 = jnp.max(jnp.where(is1, NEG_INF, blk), axis=0, keepdims=True)
        gscore = jnp.where(grow == g, m1 + jnp.where(cnt >= 2.0, m1, m2), gscore)
    rank = jnp.zeros((N_GROUPS, tm), F32)
    for g in range(N_GROUPS):
        other = gscore[g:g + 1, :]
        beats = (other > gscore) | ((other == gscore) & (grow > g))
        rank = rank + beats.astype(F32)
    keep = rank < float(TOPK_GROUPS)
    xm = jnp.concatenate([jnp.where(keep[g:g + 1, :], sel[g * ge:(g + 1) * ge, :], NEG_INF)
                          for g in range(N_GROUPS)], axis=0)
    erow = lax.broadcasted_iota(jnp.int32, (N_EXPERTS, tm), 0).astype(F32)
    krow = lax.broadcasted_iota(jnp.int32, (TOP_K, tm), 0)
    idx = jnp.zeros((TOP_K, tm), F32)
    gk = jnp.zeros((TOP_K, tm), F32)
    chosen = jnp.zeros((N_EXPERTS, tm), F32)
    for k in range(TOP_K):
        m = jnp.max(xm, axis=0, keepdims=True)
        first = jnp.min(jnp.where(xm == m, erow, float(N_EXPERTS)), axis=0, keepdims=True)
        hit = erow == first
        idx = jnp.where(krow == k, first, idx)
        gk = jnp.where(krow == k, jnp.sum(jnp.where(hit, scores, 0.0), axis=0, keepdims=True), gk)
        chosen = chosen + hit.astype(F32)
        xm = jnp.where(hit, NEG_INF, xm)
    idx_ref[...] = idx.astype(jnp.int32)
    gate_ref[...] = gk / jnp.sum(gk, axis=0, keepdims=True) * ROUTED_SCALE

    @pl.when(pl.program_id(0) == 0)
    def _():
        cnt_ref[...] = jnp.zeros_like(cnt_ref)

    cnt_ref[...] += jnp.sum(chosen, axis=1, keepdims=True)


def _router(x1, mod3, g2, w_router, rb_col):
    tm = 256
    t = x1.shape[0]
    return pl.pallas_call(
        _router_kernel,
        grid=(t // tm,),
        in_specs=[pl.BlockSpec((tm, D_MODEL), lambda i: (i, 0)),
                  pl.BlockSpec((None, 6, D_MODEL), lambda i: (i // (SEQ // tm), 0, 0)),
                  pl.BlockSpec((1, D_MODEL), lambda i: (0, 0)),
                  pl.BlockSpec((D_MODEL, N_EXPERTS), lambda i: (0, 0)),
                  pl.BlockSpec((N_EXPERTS, 1), lambda i: (0, 0))],
        out_specs=[pl.BlockSpec((tm, D_MODEL), lambda i: (i, 0)),
                   pl.BlockSpec((tm, D_MODEL // 2), lambda i: (i, 0)),
                   pl.BlockSpec((TOP_K, tm), lambda i: (0, i)),
                   pl.BlockSpec((TOP_K, tm), lambda i: (0, i)),
                   pl.BlockSpec((N_EXPERTS, 1), lambda i: (0, 0))],
        out_shape=[jax.ShapeDtypeStruct((t, D_MODEL), F32),
                   jax.ShapeDtypeStruct((t, D_MODEL // 2), jnp.uint32),
                   jax.ShapeDtypeStruct((TOP_K, t), jnp.int32),
                   jax.ShapeDtypeStruct((TOP_K, t), F32),
                   jax.ShapeDtypeStruct((N_EXPERTS, 1), F32)],
        compiler_params=_cparams(("arbitrary",)),
        name="ffn_norm_router",
    )(x1, mod3, g2, w_router, rb_col)


def _experts_kernel(te_ref, first_ref, nxt_ref, valid_ref, tok_ref, wt_ref, hp_hbm, wg_hbm, wu_hbm, wd_hbm, y_ref,
                    hp_v, wg_f, wu_f, wd_f, wg_b, wu_b, wd_b, xbuf, hsem, wsem):
    tm = MOE_TILE
    i = pl.program_id(0)

    def w_copies(e):
        return (pltpu.make_async_copy(wg_hbm.at[e], wg_f, wsem.at[0]),
                pltpu.make_async_copy(wu_hbm.at[e], wu_f, wsem.at[1]),
                pltpu.make_async_copy(wd_hbm.at[e], wd_f, wsem.at[2]))

    @pl.when(i == 0)
    def _():
        resident = pltpu.make_async_copy(hp_hbm, hp_v, hsem)
        resident.start()
        for cp in w_copies(te_ref[0]):
            cp.start()
        resident.wait()

    @pl.when(first_ref[i] == 1)
    def _():
        for cp in w_copies(te_ref[i]):
            cp.wait()
        wg_b[...] = wg_f[...].astype(BF16)
        wu_b[...] = wu_f[...].astype(BF16)
        wd_b[...] = wd_f[...].astype(BF16)

        @pl.when(nxt_ref[i] >= 0)
        def _():
            for cp in w_copies(nxt_ref[i]):
                cp.start()

    @pl.when(valid_ref[i] == 1)
    def _():
        for r in range(tm):
            xbuf[pl.ds(r, 1), :] = hp_v[pl.ds(tok_ref[0, r], 1), :]
        packed = xbuf[...]
        lo = lax.bitcast_convert_type(lax.shift_left(packed, jnp.uint32(16)), F32).astype(BF16)
        hi = lax.bitcast_convert_type(packed & jnp.uint32(0xFFFF0000), F32).astype(BF16)
        x = jnp.concatenate([lo, hi], axis=1)
        hid = (_silu(_dot(x, wg_b[...])) * _dot(x, wu_b[...])).astype(BF16)
        y_ref[...] = _dot(hid, wd_b[...]) * wt_ref[...]

    @pl.when(valid_ref[i] == 0)
    def _():
        y_ref[...] = jnp.zeros_like(y_ref)


def _experts(sched, tok, wts, hp, w_gate, w_up, w_down):
    tm = MOE_TILE
    nt = tok.shape[0]
    any_spec = pl.BlockSpec(memory_space=pl.ANY)
    grid_spec = pltpu.PrefetchScalarGridSpec(
        num_scalar_prefetch=len(sched),
        grid=(nt,),
        in_specs=[pl.BlockSpec((None, 1, tm), lambda i, *_: (i, 0, 0), memory_space=pltpu.SMEM),
                  pl.BlockSpec((tm, 1), lambda i, *_: (i, 0)),
                  any_spec, any_spec, any_spec, any_spec],
        out_specs=pl.BlockSpec((tm, D_MODEL), lambda i, *_: (i, 0)),
        scratch_shapes=[pltpu.VMEM(hp.shape, jnp.uint32),
                        pltpu.VMEM((D_MODEL, D_EXPERT), F32),
                        pltpu.VMEM((D_MODEL, D_EXPERT), F32),
                        pltpu.VMEM((D_EXPERT, D_MODEL), F32),
                        pltpu.VMEM((D_MODEL, D_EXPERT), BF16),
                        pltpu.VMEM((D_MODEL, D_EXPERT), BF16),
                        pltpu.VMEM((D_EXPERT, D_MODEL), BF16),
                        pltpu.VMEM((tm, D_MODEL // 2), jnp.uint32),
                        pltpu.SemaphoreType.DMA(()),
                        pltpu.SemaphoreType.DMA((3,))],
    )
    return pl.pallas_call(
        _experts_kernel,
        grid_spec=grid_spec,
        out_shape=jax.ShapeDtypeStruct((nt * tm, D_MODEL), F32),
        compiler_params=pltpu.CompilerParams(dimension_semantics=("arbitrary",),
                                             vmem_limit_bytes=EXPERTS_VMEM_LIMIT),
        name="routed_experts",
    )(*sched, tok, wts, hp, w_gate, w_up, w_down)


def _final_kernel(pos_ref, posn_ref, x_ref, h_ref, mod_ref, g_ref, wg_ref, wu_ref, wd_ref, y_hbm, out_ref, gbuf, gsem):
    tm = x_ref.shape[0]
    i = pl.program_id(0)
    last = pl.num_programs(0) - 1
    s = i % 2

    def g_copy(k, r, slot, row):
        return pltpu.make_async_copy(y_hbm.at[pl.ds(row, 1)], gbuf.at[slot, k, pl.ds(r, 1)], gsem.at[slot])

    def gather(slot, rows_ref):
        def per_k(k, carry):
            for r in range(tm):
                g_copy(k, r, slot, rows_ref[k, r]).start()
            return carry
        lax.fori_loop(0, TOP_K, per_k, 0)

    @pl.when(i == 0)
    def _():
        gather(0, pos_ref)

    @pl.when(i < last)
    def _():
        gather(1 - s, posn_ref)

    def wait_k(k, carry):
        for r in range(tm):
            g_copy(k, r, s, 0).wait()
        return carry
    lax.fori_loop(0, TOP_K, wait_k, 0)

    h = h_ref[...].astype(BF16)
    hid = (_silu(_dot(h, wg_ref[...])) * _dot(h, wu_ref[...])).astype(BF16)
    y = _dot(hid, wd_ref[...])
    for k in range(TOP_K):
        y = y + gbuf[s, k]
    out_ref[...] = x_ref[...] + mod_ref[5:6, :] * _rms(y, g_ref[...])


def _final(x1, h2, y_rows, pos, mod3, g3, ws_gate, ws_up, ws_down):
    tm = 128
    t = x1.shape[0]
    nt = t // tm
    full = pl.BlockSpec((tm, D_MODEL), lambda i: (i, 0))

    def pos_spec(ahead):
        return pl.BlockSpec((None, TOP_K, tm), lambda i: (jnp.minimum(i + ahead, nt - 1), 0, 0),
                            memory_space=pltpu.SMEM)

    return pl.pallas_call(
        _final_kernel,
        grid=(nt,),
        in_specs=[pos_spec(0), pos_spec(1), full, full,
                  pl.BlockSpec((None, 6, D_MODEL), lambda i: (i // (SEQ // tm), 0, 0)),
                  pl.BlockSpec((1, D_MODEL), lambda i: (0, 0)),
                  pl.BlockSpec((D_MODEL, D_EXPERT), lambda i: (0, 0)),
                  pl.BlockSpec((D_MODEL, D_EXPERT), lambda i: (0, 0)),
                  pl.BlockSpec((D_EXPERT, D_MODEL), lambda i: (0, 0)),
                  pl.BlockSpec(memory_space=pl.ANY)],
        out_specs=full,
        out_shape=jax.ShapeDtypeStruct((t, D_MODEL), F32),
        scratch_shapes=[pltpu.VMEM((2, TOP_K, tm, D_MODEL), F32),
                        pltpu.SemaphoreType.DMA((2,))],
        compiler_params=_cparams(("arbitrary",)),
        name="shared_expert_final",
    )(pos, pos, x1, h2, mod3, g3, ws_gate, ws_up, ws_down, y_rows)


def _dispatch(idx_t, gate_t, counts):
    t = idx_t.shape[1]
    a = t * TOP_K
    tm = MOE_TILE
    nt = (a + N_EXPERTS * (tm - 1) + tm - 1) // tm
    e_flat = idx_t.reshape(-1)
    order = jnp.argsort(e_flat).astype(jnp.int32)
    rank = jnp.argsort(order).astype(jnp.int32)
    starts = jnp.cumsum(counts) - counts
    padded = (counts + tm - 1) // tm * tm
    pends = jnp.cumsum(padded)
    pstarts = pends - padded
    tile_start = jnp.arange(nt, dtype=jnp.int32) * tm
    te = jnp.minimum(jnp.sum(pends[None, :] <= tile_start[:, None], axis=1), N_EXPERTS - 1).astype(jnp.int32)
    valid = tile_start < pends[-1]
    lane = jnp.arange(tm, dtype=jnp.int32)[None, :]
    j = tile_start[:, None] + lane - pstarts[te][:, None]
    row_ok = (j < counts[te][:, None]) & valid[:, None]
    asg = order[jnp.clip(starts[te][:, None] + j, 0, a - 1)]
    tok = jnp.where(row_ok, asg % t, 0).astype(jnp.int32)
    wt = jnp.where(row_ok, gate_t.reshape(-1)[asg], 0.0)
    pos = (rank + (pstarts - starts)[e_flat]).astype(jnp.int32)
    tiles = jnp.arange(nt, dtype=jnp.int32)
    first = jnp.concatenate([jnp.ones((1,), bool), te[1:] != te[:-1]])
    run_start = jnp.where(first, tiles, nt)
    next_start = jnp.concatenate([lax.cummin(run_start[::-1])[::-1][1:], jnp.full((1,), nt, jnp.int32)])
    nxt = jnp.where(next_start < nt, te[jnp.clip(next_start, 0, nt - 1)], -1).astype(jnp.int32)
    sched = (te, first.astype(jnp.int32), nxt, valid.astype(jnp.int32))
    pos_tiles = pos.reshape(TOP_K, t // tm, tm).transpose(1, 0, 2)
    return sched, tok[:, None, :], wt.reshape(-1, 1), pos_tiles


def kernel(x, c, w_ada, b_ada, norm_gains, w_in, b_if, conv_w, conv_b, head_gains, rel_bias, w_out,
           w_router, router_bias, w_exp_gate, w_exp_up, w_exp_down, w_sh_gate, w_sh_up, w_sh_down):
    b, s, d = x.shape
    t = b * s
    x2 = x.reshape(t, d)
    for l in range(w_ada.shape[0]):
        c_pad = jnp.pad(c, ((0, 8 - b), (0, 0)))
        mod = _ada(c_pad, w_ada[l], b_ada[l][None, :])[:b]
        mod3 = mod.reshape(b, 6, d)
        g = norm_gains[l]
        w_gates = jnp.pad(w_in[l][:, D_PROJ_MAIN:], ((0, 0), (0, LANES - N_GATES)))
        bif = jnp.pad(b_if[l][None, :], ((0, 0), (0, LANES - N_GATES)))
        proj, gates = _inproj(x2, mod3, g[0][None, :], w_in[l], w_gates, bif)
        gates_t = gates[:, :N_GATES].reshape(b, s, N_GATES).transpose(0, 2, 1)
        hm = _mlstm(proj, gates, gates_t, conv_w[l], conv_b[l][None, :])
        ha = _attn(proj, _attn_bias(rel_bias))
        x1 = _outproj(hm, proj, ha, x2, mod3, head_gains[l][None, :], g[1][None, :], w_out[l].astype(BF16))
        h2, hp, idx_t, gate_t, cnt = _router(x1, mod3, g[2][None, :], w_router[l], router_bias[l][:, None])
        sched, tok, wts, pos = _dispatch(idx_t, gate_t, cnt[:, 0].astype(jnp.int32))
        y_rows = _experts(sched, tok, wts, hp, w_exp_gate[l], w_exp_up[l], w_exp_down[l])
        x2 = _final(x1, h2, y_rows, pos, mod3, g[3][None, :], w_sh_gate[l].astype(BF16), w_sh_up[l].astype(BF16),
                    w_sh_down[l].astype(BF16))
    return x2.reshape(b, s, d)
```

```python
import numpy as np
import jax
import jax.numpy as jnp
from jax import lax
from jax.experimental import pallas as pl
from jax.experimental.pallas import tpu as pltpu

D_MODEL = 2048
SEQ = 2048
HEAD_DIM = 128
M_HEADS = 8
A_HEADS = 8
D_M = M_HEADS * HEAD_DIM
D_A = A_HEADS * HEAD_DIM
CONV_K = 4
MLSTM_CHUNK = 128
DILATED_PATTERNS = ((128, 1), (512, 4), (2048, 16))
REL_BUCKETS = 32
REL_MAX_DIST = 2048
N_EXPERTS = 256
TOP_K = 8
N_GROUPS = 8
TOPK_GROUPS = 4
D_EXPERT = 512
ROUTED_SCALE = 2.5
NORM_EPS = 1e-6
D_PROJ_MAIN = 4 * D_M + 3 * D_A
N_GATES = 2 * M_HEADS
LANES = 128
ATT_W = 128
MOE_TILE = 128
VMEM_LIMIT = 56 * 1024 * 1024
EXPERTS_VMEM_LIMIT = 60 * 1024 * 1024

F32 = jnp.float32
BF16 = jnp.bfloat16
NEG_INF = float("-inf")


def _cparams(sem):
    return pltpu.CompilerParams(dimension_semantics=sem, vmem_limit_bytes=VMEM_LIMIT)


def _dot(a, b):
    return jnp.dot(a, b, preferred_element_type=F32)


def _dot_nt(a, b):
    return lax.dot_general(a, b, (((1,), (1,)), ((), ())), preferred_element_type=F32)


def _dot_tn(a, b):
    return lax.dot_general(a, b, (((0,), (0,)), ((), ())), preferred_element_type=F32)


def _rms(x, g):
    ms = jnp.mean(x * x, axis=-1, keepdims=True)
    return x * lax.rsqrt(ms + NORM_EPS) * g


def _silu(x):
    return x * jax.nn.sigmoid(x)


def _ada_kernel(c_ref, w_ref, b_ref, o_ref):
    a = _silu(c_ref[...]).astype(BF16)
    o_ref[...] = _dot(a, w_ref[...].astype(BF16)) + b_ref[...]


def _ada(c_pad, w_ada, b_ada):
    tn = 1024
    n = w_ada.shape[1]
    return pl.pallas_call(
        _ada_kernel,
        grid=(n // tn,),
        in_specs=[pl.BlockSpec((8, D_MODEL), lambda j: (0, 0)),
                  pl.BlockSpec((D_MODEL, tn), lambda j: (0, j)),
                  pl.BlockSpec((1, tn), lambda j: (0, j))],
        out_specs=pl.BlockSpec((8, tn), lambda j: (0, j)),
        out_shape=jax.ShapeDtypeStruct((8, n), F32),
        compiler_params=_cparams(("arbitrary",)),
        name="ada_mod",
    )(c_pad, w_ada, b_ada)


def _inproj_kernel(x_ref, mod_ref, g_ref, w_ref, wg_ref, bif_ref, proj_ref, gates_ref, h_sc):
    @pl.when(pl.program_id(1) == 0)
    def _():
        h = _rms(x_ref[...], g_ref[...]) * (1.0 + mod_ref[1:2, :]) + mod_ref[0:1, :]
        hb = h.astype(BF16)
        h_sc[...] = hb
        g = _dot(hb, wg_ref[...].astype(BF16)) + bif_ref[...]
        lane = lax.broadcasted_iota(jnp.int32, g.shape, 1)
        logf = jnp.minimum(g, 0.0) - jnp.log1p(jnp.exp(-jnp.abs(g)))
        gates_ref[...] = jnp.where(lane < M_HEADS, g, logf)

    proj_ref[...] = _dot(h_sc[...], w_ref[...].astype(BF16))


def _inproj(x2, mod3, g0, w_in, w_gates, bif):
    tm, tn = 1024, 512
    t = x2.shape[0]
    return pl.pallas_call(
        _inproj_kernel,
        grid=(t // tm, D_PROJ_MAIN // tn),
        in_specs=[pl.BlockSpec((tm, D_MODEL), lambda i, j: (i, 0)),
                  pl.BlockSpec((None, 6, D_MODEL), lambda i, j: (i // (SEQ // tm), 0, 0)),
                  pl.BlockSpec((1, D_MODEL), lambda i, j: (0, 0)),
                  pl.BlockSpec((D_MODEL, tn), lambda i, j: (0, j)),
                  pl.BlockSpec((D_MODEL, LANES), lambda i, j: (0, 0)),
                  pl.BlockSpec((1, LANES), lambda i, j: (0, 0))],
        out_specs=[pl.BlockSpec((tm, tn), lambda i, j: (i, j)),
                   pl.BlockSpec((tm, LANES), lambda i, j: (i, 0))],
        out_shape=[jax.ShapeDtypeStruct((t, D_PROJ_MAIN), F32),
                   jax.ShapeDtypeStruct((t, LANES), F32)],
        scratch_shapes=[pltpu.VMEM((tm, D_MODEL), BF16)],
        compiler_params=_cparams(("arbitrary", "arbitrary")),
        name="in_proj",
    )(x2, mod3, g0, w_in, w_gates, bif)


def _mlstm_kernel(q_ref, k_ref, v_ref, g_ref, gt_ref, cwq_ref, cwk_ref, cbq_ref, cbk_ref, o_ref,
                  ct_sc, n_sc, m_sc, pq_sc, pk_sc):
    lc = MLSTM_CHUNK

    @pl.when(pl.program_id(1) == 0)
    def _():
        ct_sc[...] = jnp.zeros_like(ct_sc)
        n_sc[...] = jnp.zeros_like(n_sc)
        m_sc[...] = jnp.zeros_like(m_sc)
        pq_sc[...] = jnp.zeros_like(pq_sc)
        pk_sc[...] = jnp.zeros_like(pk_sc)

    row_w = lax.broadcasted_iota(jnp.int32, (lc, D_M), 0)

    def conv(cur, prev, w_ref, b_ref):
        acc = cur * w_ref[CONV_K - 1:CONV_K, :] + b_ref[...]
        for s in range(1, CONV_K):
            sh = jnp.where(row_w >= s, pltpu.roll(cur, s, 0), pltpu.roll(prev, s, 0))
            acc = acc + sh * w_ref[CONV_K - 1 - s:CONV_K - s, :]
        return _silu(acc)

    q_raw = q_ref[...]
    k_raw = k_ref[...]
    q_all = conv(q_raw, pq_sc[...], cwq_ref, cbq_ref)
    k_all = conv(k_raw, pk_sc[...], cwk_ref, cbk_ref) * (HEAD_DIM ** -0.5)
    pq_sc[...] = q_raw
    pk_sc[...] = k_raw

    row = lax.broadcasted_iota(jnp.int32, (lc, lc), 0)
    col = lax.broadcasted_iota(jnp.int32, (lc, lc), 1)
    causal = col <= row
    tril = causal.astype(F32)
    triu = (row <= col).astype(F32)
    g = g_ref[...]
    gt = gt_ref[...]
    b_cols = jnp.dot(tril, g, precision=lax.Precision.HIGHEST, preferred_element_type=F32)
    b_rows = jnp.dot(gt, triu, precision=lax.Precision.HIGHEST, preferred_element_type=F32)

    for h in range(M_HEADS):
        sl = slice(h * HEAD_DIM, (h + 1) * HEAD_DIM)
        i_col = g[:, h:h + 1]
        b_col = b_cols[:, M_HEADS + h:M_HEADS + h + 1]
        i_row = gt[h:h + 1, :]
        b_row = b_rows[M_HEADS + h:M_HEADS + h + 1, :]
        m_prev = m_sc[h:h + 1, 0:1]
        n_prev = n_sc[h:h + 1, :]
        ct_prev = ct_sc[h]
        qh = q_all[:, sl]
        kh = k_all[:, sl]
        vh = v_ref[:, sl]
        qb = qh.astype(BF16)
        kb = kh.astype(BF16)

        dmat = jnp.where(causal, b_col - b_row + i_row, NEG_INF)
        inter = b_col + m_prev
        m_t = jnp.maximum(inter, jnp.max(dmat, axis=-1, keepdims=True))
        w_intra = jnp.exp(dmat - m_t)
        w_inter = jnp.exp(inter - m_t)
        s = _dot_nt(qb, kb) * w_intra
        num = _dot(s.astype(BF16), vh.astype(BF16)) + w_inter * _dot(qb, ct_prev.astype(BF16))
        den = jnp.sum(s, axis=-1, keepdims=True) + w_inter * jnp.sum(qh * n_prev, axis=-1, keepdims=True)
        o_ref[:, sl] = num / jnp.maximum(jnp.abs(den), jnp.exp(-m_t))

        b_last = b_col[lc - 1:lc, :]
        log_w = b_last - b_col + i_col
        m_new = jnp.maximum(b_last + m_prev, jnp.max(log_w, axis=0, keepdims=True))
        w_upd = jnp.exp(log_w - m_new)
        decay = jnp.exp(b_last + m_prev - m_new)
        ct_sc[h] = decay * ct_prev + _dot_tn(kb, (vh * w_upd).astype(BF16))
        n_sc[h:h + 1, :] = decay * n_prev + jnp.sum(kh * w_upd, axis=0, keepdims=True)
        m_sc[h:h + 1, :] = jnp.broadcast_to(m_new, (1, LANES))


def _mlstm(proj, gates, gates_t, conv_w, conv_b):
    lc = MLSTM_CHUNK
    nc = SEQ // lc
    t = proj.shape[0]
    return pl.pallas_call(
        _mlstm_kernel,
        grid=(t // SEQ, nc),
        in_specs=[pl.BlockSpec((lc, D_M), lambda b, c: (b * nc + c, 0)),
                  pl.BlockSpec((lc, D_M), lambda b, c: (b * nc + c, 1)),
                  pl.BlockSpec((lc, D_M), lambda b, c: (b * nc + c, 2)),
                  pl.BlockSpec((lc, LANES), lambda b, c: (b * nc + c, 0)),
                  pl.BlockSpec((None, N_GATES, lc), lambda b, c: (b, 0, c)),
                  pl.BlockSpec((CONV_K, D_M), lambda b, c: (0, 0)),
                  pl.BlockSpec((CONV_K, D_M), lambda b, c: (0, 1)),
                  pl.BlockSpec((1, D_M), lambda b, c: (0, 0)),
                  pl.BlockSpec((1, D_M), lambda b, c: (0, 1))],
        out_specs=pl.BlockSpec((lc, D_M), lambda b, c: (b * nc + c, 0)),
        out_shape=jax.ShapeDtypeStruct((t, D_M), F32),
        scratch_shapes=[pltpu.VMEM((M_HEADS, HEAD_DIM, HEAD_DIM), F32),
                        pltpu.VMEM((M_HEADS, HEAD_DIM), F32),
                        pltpu.VMEM((M_HEADS, LANES), F32),
                        pltpu.VMEM((lc, D_M), F32),
                        pltpu.VMEM((lc, D_M), F32)],
        compiler_params=_cparams(("arbitrary", "arbitrary")),
        name="mlstm",
    )(proj, proj, proj, gates, gates_t, conv_w, conv_w, conv_b, conv_b)


def _attn_kernel(q_ref, k_ref, v_ref, bias_ref, out_ref, o_sc, l_sc, bias_sc):
    w = ATT_W
    row = lax.broadcasted_iota(jnp.int32, (w, w), 0)
    col = lax.broadcasted_iota(jnp.int32, (w, w), 1)
    cur_ok = col <= row
    scale = HEAD_DIM ** -0.5
    for p in range(len(DILATED_PATTERNS)):
        bias_sc[p] = pltpu.roll(jnp.broadcast_to(bias_ref[p], (w, 2 * w)), 0, 1, stride=1, stride_axis=0)
    for p, (_, d) in enumerate(DILATED_PATTERNS):
        nb = SEQ // d // w
        nb_shift = nb.bit_length() - 1

        def rows(start, d=d):
            return pl.ds(start, w) if d == 1 else pl.ds(start, w, stride=d)

        def block(i, carry, p=p, d=d, nb=nb, nb_shift=nb_shift, rows=rows):
            r = lax.shift_right_logical(i, nb_shift)
            n = jnp.bitwise_and(i, nb - 1)
            start = r + n * (w * d)
            qb = q_ref[rows(start), :].astype(BF16)
            s_c = _dot_nt(qb, k_ref[rows(start), :].astype(BF16)) * scale + bias_sc[p, :, w:2 * w]
            s_c = jnp.where(cur_ok, s_c, NEG_INF)
            m = jnp.max(s_c, axis=-1, keepdims=True)
            if nb > 1:
                pstart = r + jnp.maximum(n - 1, 0) * (w * d)
                s_p = _dot_nt(qb, k_ref[rows(pstart), :].astype(BF16)) * scale + bias_sc[p, :, 0:w]
                s_p = jnp.where((col >= row) & (n > 0), s_p, NEG_INF)
                m = jnp.maximum(m, jnp.max(s_p, axis=-1, keepdims=True))
            p_c = jnp.exp(s_c - m)
            den = jnp.sum(p_c, axis=-1, keepdims=True)
            acc = _dot(p_c.astype(BF16), v_ref[rows(start), :].astype(BF16))
            if nb > 1:
                p_p = jnp.exp(s_p - m)
                den = den + jnp.sum(p_p, axis=-1, keepdims=True)
                acc = acc + _dot(p_p.astype(BF16), v_ref[rows(pstart), :].astype(BF16))
            o_sc[p, rows(start), :] = acc / den
            l_sc[p, rows(start), :] = jnp.broadcast_to(m + jnp.log(den), (w, HEAD_DIM))
            return carry

        lax.fori_loop(0, d * nb, block, 0)

    chunk = 256

    def combine(c, carry):
        rs = pl.ds(pl.multiple_of(c * chunk, chunk), chunk)
        l1, l2, l3 = l_sc[0, rs, :], l_sc[1, rs, :], l_sc[2, rs, :]
        lm = jnp.maximum(jnp.maximum(l1, l2), l3)
        e1, e2, e3 = jnp.exp(l1 - lm), jnp.exp(l2 - lm), jnp.exp(l3 - lm)
        tot = e1 + e2 + e3
        out_ref[rs, :] = (e1 / tot) * o_sc[0, rs, :] + (e2 / tot) * o_sc[1, rs, :] + (e3 / tot) * o_sc[2, rs, :]
        return carry

    lax.fori_loop(0, SEQ // chunk, combine, 0)


def _attn(proj, bias_all):
    t = proj.shape[0]
    nblk = D_PROJ_MAIN // HEAD_DIM
    qi, ki, vi = nblk - 3 * A_HEADS, nblk - 2 * A_HEADS, nblk - A_HEADS
    npat = len(DILATED_PATTERNS)

    def head(base):
        return pl.BlockSpec((SEQ, HEAD_DIM), lambda b, h: (b, base + h))

    return pl.pallas_call(
        _attn_kernel,
        grid=(t // SEQ, A_HEADS),
        in_specs=[head(qi), head(ki), head(vi),
                  pl.BlockSpec((npat, None, 1, 2 * ATT_W), lambda b, h: (0, h, 0, 0))],
        out_specs=pl.BlockSpec((SEQ, HEAD_DIM), lambda b, h: (b, h)),
        out_shape=jax.ShapeDtypeStruct((t, D_A), F32),
        scratch_shapes=[pltpu.VMEM((npat, SEQ, HEAD_DIM), F32),
                        pltpu.VMEM((npat, SEQ, HEAD_DIM), F32),
                        pltpu.VMEM((npat, ATT_W, 2 * ATT_W), F32)],
        compiler_params=_cparams(("arbitrary", "arbitrary")),
        name="dilated_attn",
    )(proj, proj, proj, bias_all)


def _t5_causal_bucket(dist):
    max_exact = REL_BUCKETS // 2
    d = np.maximum(dist, 1).astype(np.float32)
    large = max_exact + (np.log(d / max_exact) / np.log(REL_MAX_DIST / max_exact)
                         * (REL_BUCKETS - max_exact)).astype(np.int32)
    return np.where(dist < max_exact, dist, np.minimum(large, REL_BUCKETS - 1)).astype(np.int32)


def _attn_bias(rel_bias):
    w = ATT_W
    j = np.clip(w - np.arange(2 * w), 0, w)
    tabs = [rel_bias[_t5_causal_bucket(j * dil)] for _, dil in DILATED_PATTERNS]
    return jnp.transpose(jnp.stack(tabs), (0, 2, 1))[:, :, None, :].astype(F32)


def _head_rms(x, g_ref, base):
    outs = []
    for h in range(x.shape[1] // HEAD_DIM):
        sl = slice(h * HEAD_DIM, (h + 1) * HEAD_DIM)
        outs.append(_rms(x[:, sl], g_ref[:, base + h * HEAD_DIM:base + (h + 1) * HEAD_DIM]))
    return outs


def _outproj_kernel(hm_ref, om_ref, ha_ref, x_ref, mod_ref, hg_ref, g_ref, w_ref, out_ref, cat_sc):
    gate = jax.nn.sigmoid(om_ref[...])
    for h, seg in enumerate(_head_rms(hm_ref[...], hg_ref, 0)):
        sl = slice(h * HEAD_DIM, (h + 1) * HEAD_DIM)
        cat_sc[:, sl] = (seg * gate[:, sl]).astype(BF16)
    for h, seg in enumerate(_head_rms(ha_ref[...], hg_ref, D_M)):
        cat_sc[:, D_M + h * HEAD_DIM:D_M + (h + 1) * HEAD_DIM] = seg.astype(BF16)
    y = _dot(cat_sc[...], w_ref[...])
    out_ref[...] = x_ref[...] + mod_ref[2:3, :] * _rms(y, g_ref[...])


def _outproj(hm, proj, ha, x2, mod3, head_gains, g1, w_out_bf):
    tm = 256
    t = x2.shape[0]
    half = lambda blk: pl.BlockSpec((tm, D_M), lambda i: (i, blk))
    full = pl.BlockSpec((tm, D_MODEL), lambda i: (i, 0))
    vec = pl.BlockSpec((1, D_MODEL), lambda i: (0, 0))
    return pl.pallas_call(
        _outproj_kernel,
        grid=(t // tm,),
        in_specs=[half(0), half(3), half(0), full,
                  pl.BlockSpec((None, 6, D_MODEL), lambda i: (i // (SEQ // tm), 0, 0)),
                  vec, vec,
                  pl.BlockSpec((D_MODEL, D_MODEL), lambda i: (0, 0))],
        out_specs=full,
        out_shape=jax.ShapeDtypeStruct((t, D_MODEL), F32),
        scratch_shapes=[pltpu.VMEM((tm, D_MODEL), BF16)],
        compiler_params=_cparams(("arbitrary",)),
        name="out_proj",
    )(hm, proj, ha, x2, mod3, head_gains, g1, w_out_bf)


def _router_kernel(x_ref, mod_ref, g_ref, w_ref, rb_ref, h_ref, hp_ref, idx_ref, gate_ref, cnt_ref):
    tm = x_ref.shape[0]
    h = _rms(x_ref[...], g_ref[...]) * (1.0 + mod_ref[4:5, :]) + mod_ref[3:4, :]
    h_ref[...] = h
    hb = h.astype(BF16)
    bits = lax.bitcast_convert_type(hb.astype(F32), jnp.uint32)
    hp_ref[...] = bits[:, D_MODEL // 2:] | lax.shift_right_logical(bits[:, :D_MODEL // 2], jnp.uint32(16))
    scores = jax.nn.sigmoid(_dot(hb, w_ref[...].astype(BF16))).T
    sel = scores + rb_ref[...]
    ge = N_EXPERTS // N_GROUPS
    grow = lax.broadcasted_iota(jnp.int32, (N_GROUPS, tm), 0)
    gscore = jnp.zeros((N_GROUPS, tm), F32)
    for g in range(N_GROUPS):
        blk = sel[g * ge:(g + 1) * ge, :]
        m1 = jnp.max(blk, axis=0, keepdims=True)
        is1 = blk == m1
        cnt = jnp.sum(is1.astype(F32), axis=0, keepdims=True)
        m2 = jnp.max(jnp.where(is1, NEG_INF, blk), axis=0, keepdims=True)
        gscore = jnp.where(grow == g, m1 + jnp.where(cnt >= 2.0, m1, m2), gscore)
    rank = jnp.zeros((N_GROUPS, tm), F32)
    for g in range(N_GROUPS):
        other = gscore[g:g + 1, :]
        beats = (other > gscore) | ((other == gscore) & (grow > g))
        rank = rank + beats.astype(F32)
    keep = rank < float(TOPK_GROUPS)
    xm = jnp.concatenate([jnp.where(keep[g:g + 1, :], sel[g * ge:(g + 1) * ge, :], NEG_INF)
                          for g in range(N_GROUPS)], axis=0)
    erow = lax.broadcasted_iota(jnp.int32, (N_EXPERTS, tm), 0).astype(F32)
    krow = lax.broadcasted_iota(jnp.int32, (TOP_K, tm), 0)
    idx = jnp.zeros((TOP_K, tm), F32)
    gk = jnp.zeros((TOP_K, tm), F32)
    chosen = jnp.zeros((N_EXPERTS, tm), F32)
    for k in range(TOP_K):
        m = jnp.max(xm, axis=0, keepdims=True)
        first = jnp.min(jnp.where(xm == m, erow, float(N_EXPERTS)), axis=0, keepdims=True)
        hit = erow == first
        idx = jnp.where(krow == k, first, idx)
        gk = jnp.where(krow == k, jnp.sum(jnp.where(hit, scores, 0.0), axis=0, keepdims=True), gk)
        chosen = chosen + hit.astype(F32)
        xm = jnp.where(hit, NEG_INF, xm)
    idx_ref[...] = idx.astype(jnp.int32)
    gate_ref[...] = gk / jnp.sum(gk, axis=0, keepdims=True) * ROUTED_SCALE

    @pl.when(pl.program_id(0) == 0)
    def _():
        cnt_ref[...] = jnp.zeros_like(cnt_ref)

    cnt_ref[...] += jnp.sum(chosen, axis=1, keepdims=True)


def _router(x1, mod3, g2, w_router, rb_col):
    tm = 256
    t = x1.shape[0]
    return pl.pallas_call(
        _router_kernel,
        grid=(t // tm,),
        in_specs=[pl.BlockSpec((tm, D_MODEL), lambda i: (i, 0)),
                  pl.BlockSpec((None, 6, D_MODEL), lambda i: (i // (SEQ // tm), 0, 0)),
                  pl.BlockSpec((1, D_MODEL), lambda i: (0, 0)),
                  pl.BlockSpec((D_MODEL, N_EXPERTS), lambda i: (0, 0)),
                  pl.BlockSpec((N_EXPERTS, 1), lambda i: (0, 0))],
        out_specs=[pl.BlockSpec((tm, D_MODEL), lambda i: (i, 0)),
                   pl.BlockSpec((tm, D_MODEL // 2), lambda i: (i, 0)),
                   pl.BlockSpec((TOP_K, tm), lambda i: (0, i)),
                   pl.BlockSpec((TOP_K, tm), lambda i: (0, i)),
                   pl.BlockSpec((N_EXPERTS, 1), lambda i: (0, 0))],
        out_shape=[jax.ShapeDtypeStruct((t, D_MODEL), F32),
                   jax.ShapeDtypeStruct((t, D_MODEL // 2), jnp.uint32),
                   jax.ShapeDtypeStruct((TOP_K, t), jnp.int32),
                   jax.ShapeDtypeStruct((TOP_K, t), F32),
                   jax.ShapeDtypeStruct((N_EXPERTS, 1), F32)],
        compiler_params=_cparams(("arbitrary",)),
        name="ffn_norm_router",
    )(x1, mod3, g2, w_router, rb_col)


CONVERT_ROWS_BYTES = 512 * 1024


def _experts_kernel(te_ref, first_ref, nxt_ref, valid_ref, off_ref, toka_ref, tokb_ref, hp_hbm, wg_hbm, wu_hbm, wd_hbm,
                    y_ref, hp_v, wg_f, wu_f, wd_f, wg_b, wu_b, wd_b, xbuf, hsem, wsem):
    tm = MOE_TILE
    i = pl.program_id(0)
    stages = ((wg_hbm, wg_f, wg_b), (wu_hbm, wu_f, wu_b), (wd_hbm, wd_f, wd_b))

    def w_copy(m, e):
        hbm, staging, _ = stages[m]
        return pltpu.make_async_copy(hbm.at[e], staging, wsem.at[m])

    @pl.when(i == 0)
    def _():
        resident = pltpu.make_async_copy(hp_hbm, hp_v, hsem)
        resident.start()
        for m in range(len(stages)):
            w_copy(m, te_ref[0]).start()
        resident.wait()

    @pl.when(first_ref[i] == 1)
    def _():
        has_next = nxt_ref[i] >= 0
        e_next = jnp.maximum(nxt_ref[i], 0)
        for m, (_, staging, cache) in enumerate(stages):
            w_copy(m, te_ref[i]).wait()
            rows = CONVERT_ROWS_BYTES // (staging.shape[1] * 4)
            for c in range(staging.shape[0] // rows):
                cache[c * rows:(c + 1) * rows, :] = staging[c * rows:(c + 1) * rows, :].astype(BF16)

            @pl.when(has_next)
            def _(m=m):
                w_copy(m, e_next).start()

    @pl.when(valid_ref[i] == 1)
    def _():
        base = off_ref[i] % LANES
        for r in range(tm):
            j = base + r
            tok = jnp.where(j < LANES, toka_ref[0, jnp.minimum(j, LANES - 1)], tokb_ref[0, jnp.maximum(j - LANES, 0)])
            xbuf[pl.ds(r, 1), :] = hp_v[pl.ds(tok, 1), :]
        packed = xbuf[...]
        lo = lax.bitcast_convert_type(lax.shift_left(packed, jnp.uint32(16)), F32).astype(BF16)
        hi = lax.bitcast_convert_type(packed & jnp.uint32(0xFFFF0000), F32).astype(BF16)
        x = jnp.concatenate([lo, hi], axis=1)
        hid = (_silu(_dot(x, wg_b[...])) * _dot(x, wu_b[...])).astype(BF16)
        y_ref[...] = _dot(hid, wd_b[...])

    @pl.when(valid_ref[i] == 0)
    def _():
        y_ref[...] = jnp.zeros_like(y_ref)


def _experts(sched, tok_rows, hp, w_gate, w_up, w_down):
    tm = MOE_TILE
    nt = sched[0].shape[0]
    any_spec = pl.BlockSpec(memory_space=pl.ANY)

    def tok_spec(ahead):
        return pl.BlockSpec((None, 1, LANES), lambda i, te, first, nxt, valid, off: (off[i] // LANES + ahead, 0, 0),
                            memory_space=pltpu.SMEM)

    grid_spec = pltpu.PrefetchScalarGridSpec(
        num_scalar_prefetch=len(sched),
        grid=(nt,),
        in_specs=[tok_spec(0), tok_spec(1), any_spec, any_spec, any_spec, any_spec],
        out_specs=pl.BlockSpec((tm, D_MODEL), lambda i, *_: (i, 0)),
        scratch_shapes=[pltpu.VMEM(hp.shape, jnp.uint32),
                        pltpu.VMEM((D_MODEL, D_EXPERT), F32),
                        pltpu.VMEM((D_MODEL, D_EXPERT), F32),
                        pltpu.VMEM((D_EXPERT, D_MODEL), F32),
                        pltpu.VMEM((D_MODEL, D_EXPERT), BF16),
                        pltpu.VMEM((D_MODEL, D_EXPERT), BF16),
                        pltpu.VMEM((D_EXPERT, D_MODEL), BF16),
                        pltpu.VMEM((tm, D_MODEL // 2), jnp.uint32),
                        pltpu.SemaphoreType.DMA(()),
                        pltpu.SemaphoreType.DMA((3,))],
    )
    return pl.pallas_call(
        _experts_kernel,
        grid_spec=grid_spec,
        out_shape=jax.ShapeDtypeStruct((nt * tm, D_MODEL), F32),
        compiler_params=pltpu.CompilerParams(dimension_semantics=("arbitrary",),
                                             vmem_limit_bytes=EXPERTS_VMEM_LIMIT),
        name="routed_experts",
    )(*sched, tok_rows, tok_rows, hp, w_gate, w_up, w_down)


def _final_kernel(pos_ref, posn_ref, x_ref, h_ref, gate_ref, mod_ref, g_ref, wg_ref, wu_ref, wd_ref, y_hbm, out_ref,
                  gbuf, gsem):
    tm = x_ref.shape[0]
    i = pl.program_id(0)
    last = pl.num_programs(0) - 1
    s = i % 2

    def g_copy(k, r, slot, row):
        return pltpu.make_async_copy(y_hbm.at[pl.ds(row, 1)], gbuf.at[slot, k, pl.ds(r, 1)], gsem.at[slot])

    def gather(slot, rows_ref):
        def per_k(k, carry):
            for r in range(tm):
                g_copy(k, r, slot, rows_ref[k, r]).start()
            return carry
        lax.fori_loop(0, TOP_K, per_k, 0)

    @pl.when(i == 0)
    def _():
        gather(0, pos_ref)

    @pl.when(i < last)
    def _():
        gather(1 - s, posn_ref)

    def wait_k(k, carry):
        for r in range(tm):
            g_copy(k, r, s, 0).wait()
        return carry
    lax.fori_loop(0, TOP_K, wait_k, 0)

    h = h_ref[...].astype(BF16)
    hid = (_silu(_dot(h, wg_ref[...])) * _dot(h, wu_ref[...])).astype(BF16)
    y = _dot(hid, wd_ref[...])
    gate = gate_ref[...]
    for k in range(TOP_K):
        y = y + gbuf[s, k] * gate[:, k:k + 1]
    out_ref[...] = x_ref[...] + mod_ref[5:6, :] * _rms(y, g_ref[...])


def _final(x1, h2, gates, y_rows, pos, mod3, g3, ws_gate, ws_up, ws_down):
    tm = 128
    t = x1.shape[0]
    nt = t // tm
    full = pl.BlockSpec((tm, D_MODEL), lambda i: (i, 0))

    def pos_spec(ahead):
        return pl.BlockSpec((None, TOP_K, tm), lambda i: (jnp.minimum(i + ahead, nt - 1), 0, 0),
                            memory_space=pltpu.SMEM)

    return pl.pallas_call(
        _final_kernel,
        grid=(nt,),
        in_specs=[pos_spec(0), pos_spec(1), full, full,
                  pl.BlockSpec((tm, TOP_K), lambda i: (i, 0)),
                  pl.BlockSpec((None, 6, D_MODEL), lambda i: (i // (SEQ // tm), 0, 0)),
                  pl.BlockSpec((1, D_MODEL), lambda i: (0, 0)),
                  pl.BlockSpec((D_MODEL, D_EXPERT), lambda i: (0, 0)),
                  pl.BlockSpec((D_MODEL, D_EXPERT), lambda i: (0, 0)),
                  pl.BlockSpec((D_EXPERT, D_MODEL), lambda i: (0, 0)),
                  pl.BlockSpec(memory_space=pl.ANY)],
        out_specs=full,
        out_shape=jax.ShapeDtypeStruct((t, D_MODEL), F32),
        scratch_shapes=[pltpu.VMEM((2, TOP_K, tm, D_MODEL), F32),
                        pltpu.SemaphoreType.DMA((2,))],
        compiler_params=_cparams(("arbitrary",)),
        name="shared_expert_final",
    )(pos, pos, x1, h2, gates, mod3, g3, ws_gate, ws_up, ws_down, y_rows)


def _dispatch(idx_t, counts):
    t = idx_t.shape[1]
    a = t * TOP_K
    tm = MOE_TILE
    nt = (a + N_EXPERTS * (tm - 1) + tm - 1) // tm
    e_flat = idx_t.reshape(-1)
    order = jnp.argsort(e_flat).astype(jnp.int32)
    rank = jnp.argsort(order).astype(jnp.int32)
    starts = jnp.cumsum(counts) - counts
    padded = (counts + tm - 1) // tm * tm
    pends = jnp.cumsum(padded)
    pstarts = pends - padded
    tile_start = jnp.arange(nt, dtype=jnp.int32) * tm
    te = jnp.minimum(jnp.sum(pends[None, :] <= tile_start[:, None], axis=1), N_EXPERTS - 1).astype(jnp.int32)
    valid = tile_start < pends[-1]
    off = jnp.where(valid, starts[te] + tile_start - pstarts[te], 0).astype(jnp.int32)
    shift = pstarts - starts
    experts = jnp.arange(N_EXPERTS, dtype=jnp.int32)
    pos = rank + jnp.sum(jnp.where(e_flat[:, None] == experts[None, :], shift[None, :], 0), axis=1)
    tiles = jnp.arange(nt, dtype=jnp.int32)
    first = jnp.concatenate([jnp.ones((1,), bool), te[1:] != te[:-1]])
    run_start = jnp.where(first, tiles, nt)
    next_start = jnp.concatenate([lax.cummin(run_start[::-1])[::-1][1:], jnp.full((1,), nt, jnp.int32)])
    nxt = jnp.where(next_start < nt, te[jnp.clip(next_start, 0, nt - 1)], -1).astype(jnp.int32)
    sched = (te, first.astype(jnp.int32), nxt, valid.astype(jnp.int32), off)
    tok_rows = jnp.pad((order % t).reshape(a // LANES, 1, LANES), ((0, 1), (0, 0), (0, 0)))
    pos_tiles = pos.astype(jnp.int32).reshape(TOP_K, t // tm, tm).transpose(1, 0, 2)
    return sched, tok_rows, pos_tiles


def kernel(x, c, w_ada, b_ada, norm_gains, w_in, b_if, conv_w, conv_b, head_gains, rel_bias, w_out,
           w_router, router_bias, w_exp_gate, w_exp_up, w_exp_down, w_sh_gate, w_sh_up, w_sh_down):
    b, s, d = x.shape
    t = b * s
    x2 = x.reshape(t, d)
    for l in range(w_ada.shape[0]):
        c_pad = jnp.pad(c, ((0, 8 - b), (0, 0)))
        mod = _ada(c_pad, w_ada[l], b_ada[l][None, :])[:b]
        mod3 = mod.reshape(b, 6, d)
        g = norm_gains[l]
        w_gates = jnp.pad(w_in[l][:, D_PROJ_MAIN:], ((0, 0), (0, LANES - N_GATES)))
        bif = jnp.pad(b_if[l][None, :], ((0, 0), (0, LANES - N_GATES)))
        proj, gates = _inproj(x2, mod3, g[0][None, :], w_in[l], w_gates, bif)
        gates_t = gates[:, :N_GATES].reshape(b, s, N_GATES).transpose(0, 2, 1)
        hm = _mlstm(proj, gates, gates_t, conv_w[l], conv_b[l][None, :])
        ha = _attn(proj, _attn_bias(rel_bias))
        x1 = _outproj(hm, proj, ha, x2, mod3, head_gains[l][None, :], g[1][None, :], w_out[l].astype(BF16))
        h2, hp, idx_t, gate_t, cnt = _router(x1, mod3, g[2][None, :], w_router[l], router_bias[l][:, None])
        sched, tok_rows, pos = _dispatch(idx_t, cnt[:, 0].astype(jnp.int32))
        y_rows = _experts(sched, tok_rows, hp, w_exp_gate[l], w_exp_up[l], w_exp_down[l])
        x2 = _final(x1, h2, gate_t.T, y_rows, pos, mod3, g[3][None, :], w_sh_gate[l].astype(BF16), w_sh_up[l].astype(BF16),
                    w_sh_down[l].astype(BF16))
    return x2.reshape(b, s, d)
```

```python
import numpy as np
import jax
import jax.numpy as jnp
from jax import lax
from jax.experimental import pallas as pl
from jax.experimental.pallas import tpu as pltpu

D_MODEL = 2048
SEQ = 2048
HEAD_DIM = 128
M_HEADS = 8
A_HEADS = 8
D_M = M_HEADS * HEAD_DIM
D_A = A_HEADS * HEAD_DIM
CONV_K = 4
MLSTM_CHUNK = 128
DILATED_PATTERNS = ((128, 1), (512, 4), (2048, 16))
REL_BUCKETS = 32
REL_MAX_DIST = 2048
N_EXPERTS = 256
TOP_K = 8
N_GROUPS = 8
TOPK_GROUPS = 4
D_EXPERT = 512
ROUTED_SCALE = 2.5
NORM_EPS = 1e-6
D_PROJ_MAIN = 4 * D_M + 3 * D_A
N_GATES = 2 * M_HEADS
LANES = 128
ATT_W = 128
MOE_TILE = 128
VMEM_LIMIT = 56 * 1024 * 1024
EXPERTS_VMEM_LIMIT = 60 * 1024 * 1024

F32 = jnp.float32
BF16 = jnp.bfloat16
NEG_INF = float("-inf")


def _cparams(sem):
    return pltpu.CompilerParams(dimension_semantics=sem, vmem_limit_bytes=VMEM_LIMIT)


def _dot(a, b):
    return jnp.dot(a, b, preferred_element_type=F32)


def _dot_nt(a, b):
    return lax.dot_general(a, b, (((1,), (1,)), ((), ())), preferred_element_type=F32)


def _dot_tn(a, b):
    return lax.dot_general(a, b, (((0,), (0,)), ((), ())), preferred_element_type=F32)


def _rms(x, g):
    ms = jnp.mean(x * x, axis=-1, keepdims=True)
    return x * lax.rsqrt(ms + NORM_EPS) * g


def _silu(x):
    return x * jax.nn.sigmoid(x)


def _ada_kernel(c_ref, w_ref, b_ref, o_ref):
    a = _silu(c_ref[...]).astype(BF16)
    o_ref[...] = _dot(a, w_ref[...].astype(BF16)) + b_ref[...]


def _ada(c_pad, w_ada, b_ada):
    tn = 1024
    n = w_ada.shape[1]
    return pl.pallas_call(
        _ada_kernel,
        grid=(n // tn,),
        in_specs=[pl.BlockSpec((8, D_MODEL), lambda j: (0, 0)),
                  pl.BlockSpec((D_MODEL, tn), lambda j: (0, j)),
                  pl.BlockSpec((1, tn), lambda j: (0, j))],
        out_specs=pl.BlockSpec((8, tn), lambda j: (0, j)),
        out_shape=jax.ShapeDtypeStruct((8, n), F32),
        compiler_params=_cparams(("arbitrary",)),
        name="ada_mod",
    )(c_pad, w_ada, b_ada)


def _inproj_kernel(x_ref, mod_ref, g_ref, w_ref, wg_ref, bif_ref, proj_ref, gates_ref, h_sc):
    @pl.when(pl.program_id(1) == 0)
    def _():
        h = _rms(x_ref[...], g_ref[...]) * (1.0 + mod_ref[1:2, :]) + mod_ref[0:1, :]
        hb = h.astype(BF16)
        h_sc[...] = hb
        g = _dot(hb, wg_ref[...].astype(BF16)) + bif_ref[...]
        lane = lax.broadcasted_iota(jnp.int32, g.shape, 1)
        logf = jnp.minimum(g, 0.0) - jnp.log1p(jnp.exp(-jnp.abs(g)))
        gates_ref[...] = jnp.where(lane < M_HEADS, g, logf)

    proj_ref[...] = _dot(h_sc[...], w_ref[...].astype(BF16))


def _inproj(x2, mod3, g0, w_in, w_gates, bif):
    tm, tn = 1024, 512
    t = x2.shape[0]
    return pl.pallas_call(
        _inproj_kernel,
        grid=(t // tm, D_PROJ_MAIN // tn),
        in_specs=[pl.BlockSpec((tm, D_MODEL), lambda i, j: (i, 0)),
                  pl.BlockSpec((None, 6, D_MODEL), lambda i, j: (i // (SEQ // tm), 0, 0)),
                  pl.BlockSpec((1, D_MODEL), lambda i, j: (0, 0)),
                  pl.BlockSpec((D_MODEL, tn), lambda i, j: (0, j)),
                  pl.BlockSpec((D_MODEL, LANES), lambda i, j: (0, 0)),
                  pl.BlockSpec((1, LANES), lambda i, j: (0, 0))],
        out_specs=[pl.BlockSpec((tm, tn), lambda i, j: (i, j)),
                   pl.BlockSpec((tm, LANES), lambda i, j: (i, 0))],
        out_shape=[jax.ShapeDtypeStruct((t, D_PROJ_MAIN), F32),
                   jax.ShapeDtypeStruct((t, LANES), F32)],
        scratch_shapes=[pltpu.VMEM((tm, D_MODEL), BF16)],
        compiler_params=_cparams(("arbitrary", "arbitrary")),
        name="in_proj",
    )(x2, mod3, g0, w_in, w_gates, bif)


def _mlstm_kernel(q_ref, k_ref, v_ref, g_ref, gt_ref, cwq_ref, cwk_ref, cbq_ref, cbk_ref, o_ref,
                  ct_sc, n_sc, m_sc, pq_sc, pk_sc):
    lc = MLSTM_CHUNK

    @pl.when(pl.program_id(1) == 0)
    def _():
        ct_sc[...] = jnp.zeros_like(ct_sc)
        n_sc[...] = jnp.zeros_like(n_sc)
        m_sc[...] = jnp.zeros_like(m_sc)
        pq_sc[...] = jnp.zeros_like(pq_sc)
        pk_sc[...] = jnp.zeros_like(pk_sc)

    row_w = lax.broadcasted_iota(jnp.int32, (lc, D_M), 0)

    def conv(cur, prev, w_ref, b_ref):
        acc = cur * w_ref[CONV_K - 1:CONV_K, :] + b_ref[...]
        for s in range(1, CONV_K):
            sh = jnp.where(row_w >= s, pltpu.roll(cur, s, 0), pltpu.roll(prev, s, 0))
            acc = acc + sh * w_ref[CONV_K - 1 - s:CONV_K - s, :]
        return _silu(acc)

    q_raw = q_ref[...]
    k_raw = k_ref[...]
    q_all = conv(q_raw, pq_sc[...], cwq_ref, cbq_ref)
    k_all = conv(k_raw, pk_sc[...], cwk_ref, cbk_ref) * (HEAD_DIM ** -0.5)
    pq_sc[...] = q_raw
    pk_sc[...] = k_raw

    row = lax.broadcasted_iota(jnp.int32, (lc, lc), 0)
    col = lax.broadcasted_iota(jnp.int32, (lc, lc), 1)
    causal = col <= row
    tril = causal.astype(F32)
    triu = (row <= col).astype(F32)
    g = g_ref[...]
    gt = gt_ref[...]
    b_cols = jnp.dot(tril, g, precision=lax.Precision.HIGHEST, preferred_element_type=F32)
    b_rows = jnp.dot(gt, triu, precision=lax.Precision.HIGHEST, preferred_element_type=F32)

    for h in range(M_HEADS):
        sl = slice(h * HEAD_DIM, (h + 1) * HEAD_DIM)
        i_col = g[:, h:h + 1]
        b_col = b_cols[:, M_HEADS + h:M_HEADS + h + 1]
        i_row = gt[h:h + 1, :]
        b_row = b_rows[M_HEADS + h:M_HEADS + h + 1, :]
        m_prev = m_sc[h:h + 1, 0:1]
        n_prev = n_sc[h:h + 1, :]
        ct_prev = ct_sc[h]
        qh = q_all[:, sl]
        kh = k_all[:, sl]
        vh = v_ref[:, sl]
        qb = qh.astype(BF16)
        kb = kh.astype(BF16)

        dmat = jnp.where(causal, b_col - b_row + i_row, NEG_INF)
        inter = b_col + m_prev
        m_t = jnp.maximum(inter, jnp.max(dmat, axis=-1, keepdims=True))
        w_intra = jnp.exp(dmat - m_t)
        w_inter = jnp.exp(inter - m_t)
        s = _dot_nt(qb, kb) * w_intra
        num = _dot(s.astype(BF16), vh.astype(BF16)) + w_inter * _dot(qb, ct_prev.astype(BF16))
        den = jnp.sum(s, axis=-1, keepdims=True) + w_inter * jnp.sum(qh * n_prev, axis=-1, keepdims=True)
        o_ref[:, sl] = num / jnp.maximum(jnp.abs(den), jnp.exp(-m_t))

        b_last = b_col[lc - 1:lc, :]
        log_w = b_last - b_col + i_col
        m_new = jnp.maximum(b_last + m_prev, jnp.max(log_w, axis=0, keepdims=True))
        w_upd = jnp.exp(log_w - m_new)
        decay = jnp.exp(b_last + m_prev - m_new)
        ct_sc[h] = decay * ct_prev + _dot_tn(kb, (vh * w_upd).astype(BF16))
        n_sc[h:h + 1, :] = decay * n_prev + jnp.sum(kh * w_upd, axis=0, keepdims=True)
        m_sc[h:h + 1, :] = jnp.broadcast_to(m_new, (1, LANES))


def _mlstm(proj, gates, gates_t, conv_w, conv_b):
    lc = MLSTM_CHUNK
    nc = SEQ // lc
    t = proj.shape[0]
    return pl.pallas_call(
        _mlstm_kernel,
        grid=(t // SEQ, nc),
        in_specs=[pl.BlockSpec((lc, D_M), lambda b, c: (b * nc + c, 0)),
                  pl.BlockSpec((lc, D_M), lambda b, c: (b * nc + c, 1)),
                  pl.BlockSpec((lc, D_M), lambda b, c: (b * nc + c, 2)),
                  pl.BlockSpec((lc, LANES), lambda b, c: (b * nc + c, 0)),
                  pl.BlockSpec((None, N_GATES, lc), lambda b, c: (b, 0, c)),
                  pl.BlockSpec((CONV_K, D_M), lambda b, c: (0, 0)),
                  pl.BlockSpec((CONV_K, D_M), lambda b, c: (0, 1)),
                  pl.BlockSpec((1, D_M), lambda b, c: (0, 0)),
                  pl.BlockSpec((1, D_M), lambda b, c: (0, 1))],
        out_specs=pl.BlockSpec((lc, D_M), lambda b, c: (b * nc + c, 0)),
        out_shape=jax.ShapeDtypeStruct((t, D_M), F32),
        scratch_shapes=[pltpu.VMEM((M_HEADS, HEAD_DIM, HEAD_DIM), F32),
                        pltpu.VMEM((M_HEADS, HEAD_DIM), F32),
                        pltpu.VMEM((M_HEADS, LANES), F32),
                        pltpu.VMEM((lc, D_M), F32),
                        pltpu.VMEM((lc, D_M), F32)],
        compiler_params=_cparams(("arbitrary", "arbitrary")),
        name="mlstm",
    )(proj, proj, proj, gates, gates_t, conv_w, conv_w, conv_b, conv_b)


ATT_GROUP = 4
ATT_PAD = ATT_W * 4


def _attn_kernel(q_ref, k_ref, v_ref, bias_ref, out_ref, o_sc, l_sc, bias_sc, kp_sc, vp_sc, s_sc, pa_sc, pb_sc):
    w, g = ATT_W, ATT_GROUP
    gw = g * w
    row = lax.broadcasted_iota(jnp.int32, (w, w), 0)
    col = lax.broadcasted_iota(jnp.int32, (w, w), 1)
    cur_ok = col <= row
    prev_ok = col >= row
    scale = HEAD_DIM ** -0.5

    @pl.when((pl.program_id(0) == 0) & (pl.program_id(1) == 0))
    def _():
        kp_sc[0:ATT_PAD, :] = jnp.zeros((ATT_PAD, HEAD_DIM), F32)
        vp_sc[0:ATT_PAD, :] = jnp.zeros((ATT_PAD, HEAD_DIM), F32)
        pa_sc[...] = jnp.zeros_like(pa_sc)
        pb_sc[...] = jnp.zeros_like(pb_sc)

    kp_sc[ATT_PAD:, :] = k_ref[...]
    vp_sc[ATT_PAD:, :] = v_ref[...]
    for p in range(len(DILATED_PATTERNS)):
        bias_sc[p] = pltpu.roll(jnp.broadcast_to(bias_ref[p], (w, 2 * w)), 0, 1, stride=1, stride_axis=0)

    def rows(start, size, d):
        return pl.ds(start, size) if d == 1 else pl.ds(start, size, stride=d)

    def with_ones(v):
        return jnp.concatenate([v, jnp.ones(v.shape, BF16)], axis=1)

    for p, (_, d) in enumerate(DILATED_PATTERNS):
        nb = SEQ // d // w
        if nb >= g:
            def banded(i, carry, p=p, d=d, nb=nb):
                per_res = nb // g
                r = i // per_res
                n0 = (i % per_res) * g
                qstart = r + n0 * (w * d)
                kstart = ATT_PAD + qstart - w * d
                qb = q_ref[rows(qstart, gw, d), :].astype(BF16)
                kb = kp_sc[rows(kstart, gw + w, d), :].astype(BF16)
                vb = vp_sc[rows(kstart, gw + w, d), :].astype(BF16)
                s_sc[:, 0:gw + w] = _dot_nt(qb, kb)
                ms = []
                for j in range(g):
                    sj = s_sc[j * w:(j + 1) * w, j * w:(j + 2) * w] * scale + bias_sc[p]
                    ok = jnp.concatenate([prev_ok & ((n0 + j) > 0), cur_ok], axis=1)
                    sj = jnp.where(ok, sj, NEG_INF)
                    m = jnp.max(sj, axis=-1, keepdims=True)
                    pa_sc[j * w:(j + 1) * w, j * w:(j + 2) * w] = jnp.exp(sj - m).astype(BF16)
                    ms.append(m)
                acc = _dot(pa_sc[...], with_ones(vb))
                den = acc[:, HEAD_DIM:]
                o_sc[p, rows(qstart, gw, d), :] = acc[:, :HEAD_DIM] / den
                l_sc[p, rows(qstart, gw, d), :] = jnp.concatenate(ms, axis=0) + jnp.log(den)
                return carry

            lax.fori_loop(0, d * nb // g, banded, 0)
        else:
            def diagonal(i, carry, p=p, d=d):
                starts = [i * g + j for j in range(g)]
                qb = jnp.concatenate([q_ref[rows(st, w, d), :] for st in starts], axis=0).astype(BF16)
                kb = jnp.concatenate([k_ref[rows(st, w, d), :] for st in starts], axis=0).astype(BF16)
                vb = jnp.concatenate([v_ref[rows(st, w, d), :] for st in starts], axis=0).astype(BF16)
                s_sc[:, 0:gw] = _dot_nt(qb, kb)
                ms = []
                for j in range(g):
                    sj = s_sc[j * w:(j + 1) * w, j * w:(j + 1) * w] * scale + bias_sc[p, :, w:2 * w]
                    sj = jnp.where(cur_ok, sj, NEG_INF)
                    m = jnp.max(sj, axis=-1, keepdims=True)
                    pb_sc[j * w:(j + 1) * w, j * w:(j + 1) * w] = jnp.exp(sj - m).astype(BF16)
                    ms.append(m)
                acc = _dot(pb_sc[...], with_ones(vb))
                for j, st in enumerate(starts):
                    den = acc[j * w:(j + 1) * w, HEAD_DIM:]
                    o_sc[p, rows(st, w, d), :] = acc[j * w:(j + 1) * w, :HEAD_DIM] / den
                    l_sc[p, rows(st, w, d), :] = ms[j] + jnp.log(den)
                return carry

            lax.fori_loop(0, d // g, diagonal, 0)

    chunk = 256

    def combine(c, carry):
        rs = pl.ds(pl.multiple_of(c * chunk, chunk), chunk)
        l1, l2, l3 = l_sc[0, rs, :], l_sc[1, rs, :], l_sc[2, rs, :]
        lm = jnp.maximum(jnp.maximum(l1, l2), l3)
        e1, e2, e3 = jnp.exp(l1 - lm), jnp.exp(l2 - lm), jnp.exp(l3 - lm)
        tot = e1 + e2 + e3
        out_ref[rs, :] = (e1 / tot) * o_sc[0, rs, :] + (e2 / tot) * o_sc[1, rs, :] + (e3 / tot) * o_sc[2, rs, :]
        return carry

    lax.fori_loop(0, SEQ // chunk, combine, 0)


def _attn(proj, bias_all):
    t = proj.shape[0]
    nblk = D_PROJ_MAIN // HEAD_DIM
    qi, ki, vi = nblk - 3 * A_HEADS, nblk - 2 * A_HEADS, nblk - A_HEADS
    npat = len(DILATED_PATTERNS)
    gw = ATT_GROUP * ATT_W

    def head(base):
        return pl.BlockSpec((SEQ, HEAD_DIM), lambda b, h: (b, base + h))

    return pl.pallas_call(
        _attn_kernel,
        grid=(t // SEQ, A_HEADS),
        in_specs=[head(qi), head(ki), head(vi),
                  pl.BlockSpec((npat, None, 1, 2 * ATT_W), lambda b, h: (0, h, 0, 0))],
        out_specs=pl.BlockSpec((SEQ, HEAD_DIM), lambda b, h: (b, h)),
        out_shape=jax.ShapeDtypeStruct((t, D_A), F32),
        scratch_shapes=[pltpu.VMEM((npat, SEQ, HEAD_DIM), F32),
                        pltpu.VMEM((npat, SEQ, HEAD_DIM), F32),
                        pltpu.VMEM((npat, ATT_W, 2 * ATT_W), F32),
                        pltpu.VMEM((ATT_PAD + SEQ, HEAD_DIM), F32),
                        pltpu.VMEM((ATT_PAD + SEQ, HEAD_DIM), F32),
                        pltpu.VMEM((gw, gw + ATT_W), F32),
                        pltpu.VMEM((gw, gw + ATT_W), BF16),
                        pltpu.VMEM((gw, gw), BF16)],
        compiler_params=_cparams(("arbitrary", "arbitrary")),
        name="dilated_attn",
    )(proj, proj, proj, bias_all)


def _t5_causal_bucket(dist):
    max_exact = REL_BUCKETS // 2
    d = np.maximum(dist, 1).astype(np.float32)
    large = max_exact + (np.log(d / max_exact) / np.log(REL_MAX_DIST / max_exact)
                         * (REL_BUCKETS - max_exact)).astype(np.int32)
    return np.where(dist < max_exact, dist, np.minimum(large, REL_BUCKETS - 1)).astype(np.int32)


def _attn_bias(rel_bias):
    w = ATT_W
    j = np.clip(w - np.arange(2 * w), 0, w)
    tabs = [rel_bias[_t5_causal_bucket(j * dil)] for _, dil in DILATED_PATTERNS]
    return jnp.transpose(jnp.stack(tabs), (0, 2, 1))[:, :, None, :].astype(F32)


def _head_rms(x, g_ref, base):
    outs = []
    for h in range(x.shape[1] // HEAD_DIM):
        sl = slice(h * HEAD_DIM, (h + 1) * HEAD_DIM)
        outs.append(_rms(x[:, sl], g_ref[:, base + h * HEAD_DIM:base + (h + 1) * HEAD_DIM]))
    return outs


def _outproj_kernel(hm_ref, om_ref, ha_ref, x_ref, mod_ref, hg_ref, g_ref, w_ref, out_ref, cat_sc):
    gate = jax.nn.sigmoid(om_ref[...])
    for h, seg in enumerate(_head_rms(hm_ref[...], hg_ref, 0)):
        sl = slice(h * HEAD_DIM, (h + 1) * HEAD_DIM)
        cat_sc[:, sl] = (seg * gate[:, sl]).astype(BF16)
    for h, seg in enumerate(_head_rms(ha_ref[...], hg_ref, D_M)):
        cat_sc[:, D_M + h * HEAD_DIM:D_M + (h + 1) * HEAD_DIM] = seg.astype(BF16)
    y = _dot(cat_sc[...], w_ref[...])
    out_ref[...] = x_ref[...] + mod_ref[2:3, :] * _rms(y, g_ref[...])


def _outproj(hm, proj, ha, x2, mod3, head_gains, g1, w_out_bf):
    tm = 256
    t = x2.shape[0]
    half = lambda blk: pl.BlockSpec((tm, D_M), lambda i: (i, blk))
    full = pl.BlockSpec((tm, D_MODEL), lambda i: (i, 0))
    vec = pl.BlockSpec((1, D_MODEL), lambda i: (0, 0))
    return pl.pallas_call(
        _outproj_kernel,
        grid=(t // tm,),
        in_specs=[half(0), half(3), half(0), full,
                  pl.BlockSpec((None, 6, D_MODEL), lambda i: (i // (SEQ // tm), 0, 0)),
                  vec, vec,
                  pl.BlockSpec((D_MODEL, D_MODEL), lambda i: (0, 0))],
        out_specs=full,
        out_shape=jax.ShapeDtypeStruct((t, D_MODEL), F32),
        scratch_shapes=[pltpu.VMEM((tm, D_MODEL), BF16)],
        compiler_params=_cparams(("arbitrary",)),
        name="out_proj",
    )(hm, proj, ha, x2, mod3, head_gains, g1, w_out_bf)


def _router_kernel(x_ref, mod_ref, g_ref, w_ref, rb_ref, h_ref, hp_ref, idx_ref, gate_ref, cnt_ref):
    tm = x_ref.shape[0]
    h = _rms(x_ref[...], g_ref[...]) * (1.0 + mod_ref[4:5, :]) + mod_ref[3:4, :]
    h_ref[...] = h
    hb = h.astype(BF16)
    bits = lax.bitcast_convert_type(hb.astype(F32), jnp.uint32)
    hp_ref[...] = bits[:, D_MODEL // 2:] | lax.shift_right_logical(bits[:, :D_MODEL // 2], jnp.uint32(16))
    scores = jax.nn.sigmoid(_dot(hb, w_ref[...].astype(BF16))).T
    sel = scores + rb_ref[...]
    ge = N_EXPERTS // N_GROUPS
    grow = lax.broadcasted_iota(jnp.int32, (N_GROUPS, tm), 0)
    gscore = jnp.zeros((N_GROUPS, tm), F32)
    for g in range(N_GROUPS):
        blk = sel[g * ge:(g + 1) * ge, :]
        m1 = jnp.max(blk, axis=0, keepdims=True)
        is1 = blk == m1
        cnt = jnp.sum(is1.astype(F32), axis=0, keepdims=True)
        m2 = jnp.max(jnp.where(is1, NEG_INF, blk), axis=0, keepdims=True)
        gscore = jnp.where(grow == g, m1 + jnp.where(cnt >= 2.0, m1, m2), gscore)
    rank = jnp.zeros((N_GROUPS, tm), F32)
    for g in range(N_GROUPS):
        other = gscore[g:g + 1, :]
        beats = (other > gscore) | ((other == gscore) & (grow > g))
        rank = rank + beats.astype(F32)
    keep = rank < float(TOPK_GROUPS)
    xm = jnp.concatenate([jnp.where(keep[g:g + 1, :], sel[g * ge:(g + 1) * ge, :], NEG_INF)
                          for g in range(N_GROUPS)], axis=0)
    erow = lax.broadcasted_iota(jnp.int32, (N_EXPERTS, tm), 0).astype(F32)
    krow = lax.broadcasted_iota(jnp.int32, (TOP_K, tm), 0)
    idx = jnp.zeros((TOP_K, tm), F32)
    gk = jnp.zeros((TOP_K, tm), F32)
    chosen = jnp.zeros((N_EXPERTS, tm), F32)
    for k in range(TOP_K):
        m = jnp.max(xm, axis=0, keepdims=True)
        first = jnp.min(jnp.where(xm == m, erow, float(N_EXPERTS)), axis=0, keepdims=True)
        hit = erow == first
        idx = jnp.where(krow == k, first, idx)
        gk = jnp.where(krow == k, jnp.sum(jnp.where(hit, scores, 0.0), axis=0, keepdims=True), gk)
        chosen = chosen + hit.astype(F32)
        xm = jnp.where(hit, NEG_INF, xm)
    idx_ref[...] = idx.astype(jnp.int32)
    gate_ref[...] = gk / jnp.sum(gk, axis=0, keepdims=True) * ROUTED_SCALE

    @pl.when(pl.program_id(0) == 0)
    def _():
        cnt_ref[...] = jnp.zeros_like(cnt_ref)

    cnt_ref[...] += jnp.sum(chosen, axis=1, keepdims=True)


def _router(x1, mod3, g2, w_router, rb_col):
    tm = 256
    t = x1.shape[0]
    return pl.pallas_call(
        _router_kernel,
        grid=(t // tm,),
        in_specs=[pl.BlockSpec((tm, D_MODEL), lambda i: (i, 0)),
                  pl.BlockSpec((None, 6, D_MODEL), lambda i: (i // (SEQ // tm), 0, 0)),
                  pl.BlockSpec((1, D_MODEL), lambda i: (0, 0)),
                  pl.BlockSpec((D_MODEL, N_EXPERTS), lambda i: (0, 0)),
                  pl.BlockSpec((N_EXPERTS, 1), lambda i: (0, 0))],
        out_specs=[pl.BlockSpec((tm, D_MODEL), lambda i: (i, 0)),
                   pl.BlockSpec((tm, D_MODEL // 2), lambda i: (i, 0)),
                   pl.BlockSpec((TOP_K, tm), lambda i: (0, i)),
                   pl.BlockSpec((TOP_K, tm), lambda i: (0, i)),
                   pl.BlockSpec((N_EXPERTS, 1), lambda i: (0, 0))],
        out_shape=[jax.ShapeDtypeStruct((t, D_MODEL), F32),
                   jax.ShapeDtypeStruct((t, D_MODEL // 2), jnp.uint32),
                   jax.ShapeDtypeStruct((TOP_K, t), jnp.int32),
                   jax.ShapeDtypeStruct((TOP_K, t), F32),
                   jax.ShapeDtypeStruct((N_EXPERTS, 1), F32)],
        compiler_params=_cparams(("arbitrary",)),
        name="ffn_norm_router",
    )(x1, mod3, g2, w_router, rb_col)


CONVERT_ROWS_BYTES = 512 * 1024


def _experts_kernel(te_ref, first_ref, nxt_ref, valid_ref, off_ref, toka_ref, tokb_ref, hp_hbm, wg_hbm, wu_hbm, wd_hbm,
                    y_ref, hp_v, wg_f, wu_f, wd_f, wg_b, wu_b, wd_b, xbuf, hsem, wsem):
    tm = MOE_TILE
    i = pl.program_id(0)
    stages = ((wg_hbm, wg_f, wg_b), (wu_hbm, wu_f, wu_b), (wd_hbm, wd_f, wd_b))

    def w_copy(m, e):
        hbm, staging, _ = stages[m]
        return pltpu.make_async_copy(hbm.at[e], staging, wsem.at[m])

    @pl.when(i == 0)
    def _():
        resident = pltpu.make_async_copy(hp_hbm, hp_v, hsem)
        resident.start()
        for m in range(len(stages)):
            w_copy(m, te_ref[0]).start()
        resident.wait()

    @pl.when(first_ref[i] == 1)
    def _():
        has_next = nxt_ref[i] >= 0
        e_next = jnp.maximum(nxt_ref[i], 0)
        for m, (_, staging, cache) in enumerate(stages):
            w_copy(m, te_ref[i]).wait()
            rows = CONVERT_ROWS_BYTES // (staging.shape[1] * 4)
            for c in range(staging.shape[0] // rows):
                cache[c * rows:(c + 1) * rows, :] = staging[c * rows:(c + 1) * rows, :].astype(BF16)

            @pl.when(has_next)
            def _(m=m):
                w_copy(m, e_next).start()

    @pl.when(valid_ref[i] == 1)
    def _():
        base = off_ref[i] % LANES
        for r in range(tm):
            j = base + r
            tok = jnp.where(j < LANES, toka_ref[0, jnp.minimum(j, LANES - 1)], tokb_ref[0, jnp.maximum(j - LANES, 0)])
            xbuf[pl.ds(r, 1), :] = hp_v[pl.ds(tok, 1), :]
        packed = xbuf[...]
        lo = lax.bitcast_convert_type(lax.shift_left(packed, jnp.uint32(16)), F32).astype(BF16)
        hi = lax.bitcast_convert_type(packed & jnp.uint32(0xFFFF0000), F32).astype(BF16)
        x = jnp.concatenate([lo, hi], axis=1)
        hid = (_silu(_dot(x, wg_b[...])) * _dot(x, wu_b[...])).astype(BF16)
        y_ref[...] = _dot(hid, wd_b[...])

    @pl.when(valid_ref[i] == 0)
    def _():
        y_ref[...] = jnp.zeros_like(y_ref)


def _experts(sched, tok_rows, hp, w_gate, w_up, w_down):
    tm = MOE_TILE
    nt = sched[0].shape[0]
    any_spec = pl.BlockSpec(memory_space=pl.ANY)

    def tok_spec(ahead):
        return pl.BlockSpec((None, 1, LANES), lambda i, te, first, nxt, valid, off: (off[i] // LANES + ahead, 0, 0),
                            memory_space=pltpu.SMEM)

    grid_spec = pltpu.PrefetchScalarGridSpec(
        num_scalar_prefetch=len(sched),
        grid=(nt,),
        in_specs=[tok_spec(0), tok_spec(1), any_spec, any_spec, any_spec, any_spec],
        out_specs=pl.BlockSpec((tm, D_MODEL), lambda i, *_: (i, 0)),
        scratch_shapes=[pltpu.VMEM(hp.shape, jnp.uint32),
                        pltpu.VMEM((D_MODEL, D_EXPERT), F32),
                        pltpu.VMEM((D_MODEL, D_EXPERT), F32),
                        pltpu.VMEM((D_EXPERT, D_MODEL), F32),
                        pltpu.VMEM((D_MODEL, D_EXPERT), BF16),
                        pltpu.VMEM((D_MODEL, D_EXPERT), BF16),
                        pltpu.VMEM((D_EXPERT, D_MODEL), BF16),
                        pltpu.VMEM((tm, D_MODEL // 2), jnp.uint32),
                        pltpu.SemaphoreType.DMA(()),
                        pltpu.SemaphoreType.DMA((3,))],
    )
    return pl.pallas_call(
        _experts_kernel,
        grid_spec=grid_spec,
        out_shape=jax.ShapeDtypeStruct((nt * tm, D_MODEL), F32),
        compiler_params=pltpu.CompilerParams(dimension_semantics=("arbitrary",),
                                             vmem_limit_bytes=EXPERTS_VMEM_LIMIT),
        name="routed_experts",
    )(*sched, tok_rows, tok_rows, hp, w_gate, w_up, w_down)


def _final_kernel(pos_ref, posn_ref, x_ref, h_ref, gate_ref, mod_ref, g_ref, wg_ref, wu_ref, wd_ref, y_hbm, out_ref,
                  gbuf, gsem):
    tm = x_ref.shape[0]
    i = pl.program_id(0)
    last = pl.num_programs(0) - 1
    s = i % 2

    def g_copy(k, r, slot, row):
        return pltpu.make_async_copy(y_hbm.at[pl.ds(row, 1)], gbuf.at[slot, k, pl.ds(r, 1)], gsem.at[slot])

    def gather(slot, rows_ref):
        def per_k(k, carry):
            for r in range(tm):
                g_copy(k, r, slot, rows_ref[k, r]).start()
            return carry
        lax.fori_loop(0, TOP_K, per_k, 0)

    @pl.when(i == 0)
    def _():
        gather(0, pos_ref)

    @pl.when(i < last)
    def _():
        gather(1 - s, posn_ref)

    def wait_k(k, carry):
        for r in range(tm):
            g_copy(k, r, s, 0).wait()
        return carry
    lax.fori_loop(0, TOP_K, wait_k, 0)

    h = h_ref[...].astype(BF16)
    hid = (_silu(_dot(h, wg_ref[...])) * _dot(h, wu_ref[...])).astype(BF16)
    y = _dot(hid, wd_ref[...])
    gate = gate_ref[...]
    for k in range(TOP_K):
        y = y + gbuf[s, k] * gate[:, k:k + 1]
    out_ref[...] = x_ref[...] + mod_ref[5:6, :] * _rms(y, g_ref[...])


def _final(x1, h2, gates, y_rows, pos, mod3, g3, ws_gate, ws_up, ws_down):
    tm = 128
    t = x1.shape[0]
    nt = t // tm
    full = pl.BlockSpec((tm, D_MODEL), lambda i: (i, 0))

    def pos_spec(ahead):
        return pl.BlockSpec((None, TOP_K, tm), lambda i: (jnp.minimum(i + ahead, nt - 1), 0, 0),
                            memory_space=pltpu.SMEM)

    return pl.pallas_call(
        _final_kernel,
        grid=(nt,),
        in_specs=[pos_spec(0), pos_spec(1), full, full,
                  pl.BlockSpec((tm, TOP_K), lambda i: (i, 0)),
                  pl.BlockSpec((None, 6, D_MODEL), lambda i: (i // (SEQ // tm), 0, 0)),
                  pl.BlockSpec((1, D_MODEL), lambda i: (0, 0)),
                  pl.BlockSpec((D_MODEL, D_EXPERT), lambda i: (0, 0)),
                  pl.BlockSpec((D_MODEL, D_EXPERT), lambda i: (0, 0)),
                  pl.BlockSpec((D_EXPERT, D_MODEL), lambda i: (0, 0)),
                  pl.BlockSpec(memory_space=pl.ANY)],
        out_specs=full,
        out_shape=jax.ShapeDtypeStruct((t, D_MODEL), F32),
        scratch_shapes=[pltpu.VMEM((2, TOP_K, tm, D_MODEL), F32),
                        pltpu.SemaphoreType.DMA((2,))],
        compiler_params=_cparams(("arbitrary",)),
        name="shared_expert_final",
    )(pos, pos, x1, h2, gates, mod3, g3, ws_gate, ws_up, ws_down, y_rows)


def _dispatch(idx_t, counts):
    t = idx_t.shape[1]
    a = t * TOP_K
    tm = MOE_TILE
    nt = (a + N_EXPERTS * (tm - 1) + tm - 1) // tm
    e_flat = idx_t.reshape(-1)
    order = jnp.argsort(e_flat).astype(jnp.int32)
    rank = jnp.argsort(order).astype(jnp.int32)
    starts = jnp.cumsum(counts) - counts
    padded = (counts + tm - 1) // tm * tm
    pends = jnp.cumsum(padded)
    pstarts = pends - padded
    tile_start = jnp.arange(nt, dtype=jnp.int32) * tm
    te = jnp.minimum(jnp.sum(pends[None, :] <= tile_start[:, None], axis=1), N_EXPERTS - 1).astype(jnp.int32)
    valid = tile_start < pends[-1]
    off = jnp.where(valid, starts[te] + tile_start - pstarts[te], 0).astype(jnp.int32)
    shift = pstarts - starts
    experts = jnp.arange(N_EXPERTS, dtype=jnp.int32)
    pos = rank + jnp.sum(jnp.where(e_flat[:, None] == experts[None, :], shift[None, :], 0), axis=1)
    tiles = jnp.arange(nt, dtype=jnp.int32)
    first = jnp.concatenate([jnp.ones((1,), bool), te[1:] != te[:-1]])
    run_start = jnp.where(first, tiles, nt)
    next_start = jnp.concatenate([lax.cummin(run_start[::-1])[::-1][1:], jnp.full((1,), nt, jnp.int32)])
    nxt = jnp.where(next_start < nt, te[jnp.clip(next_start, 0, nt - 1)], -1).astype(jnp.int32)
    sched = (te, first.astype(jnp.int32), nxt, valid.astype(jnp.int32), off)
    tok_rows = jnp.pad((order % t).reshape(a // LANES, 1, LANES), ((0, 1), (0, 0), (0, 0)))
    pos_tiles = pos.astype(jnp.int32).reshape(TOP_K, t // tm, tm).transpose(1, 0, 2)
    return sched, tok_rows, pos_tiles


def kernel(x, c, w_ada, b_ada, norm_gains, w_in, b_if, conv_w, conv_b, head_gains, rel_bias, w_out,
           w_router, router_bias, w_exp_gate, w_exp_up, w_exp_down, w_sh_gate, w_sh_up, w_sh_down):
    b, s, d = x.shape
    t = b * s
    x2 = x.reshape(t, d)
    for l in range(w_ada.shape[0]):
        c_pad = jnp.pad(c, ((0, 8 - b), (0, 0)))
        mod = _ada(c_pad, w_ada[l], b_ada[l][None, :])[:b]
        mod3 = mod.reshape(b, 6, d)
        g = norm_gains[l]
        w_gates = jnp.pad(w_in[l][:, D_PROJ_MAIN:], ((0, 0), (0, LANES - N_GATES)))
        bif = jnp.pad(b_if[l][None, :], ((0, 0), (0, LANES - N_GATES)))
        proj, gates = _inproj(x2, mod3, g[0][None, :], w_in[l], w_gates, bif)
        gates_t = gates[:, :N_GATES].reshape(b, s, N_GATES).transpose(0, 2, 1)
        hm = _mlstm(proj, gates, gates_t, conv_w[l], conv_b[l][None, :])
        ha = _attn(proj, _attn_bias(rel_bias))
        x1 = _outproj(hm, proj, ha, x2, mod3, head_gains[l][None, :], g[1][None, :], w_out[l].astype(BF16))
        h2, hp, idx_t, gate_t, cnt = _router(x1, mod3, g[2][None, :], w_router[l], router_bias[l][:, None])
        sched, tok_rows, pos = _dispatch(idx_t, cnt[:, 0].astype(jnp.int32))
        y_rows = _experts(sched, tok_rows, hp, w_exp_gate[l], w_exp_up[l], w_exp_down[l])
        x2 = _final(x1, h2, gate_t.T, y_rows, pos, mod3, g[3][None, :], w_sh_gate[l].astype(BF16), w_sh_up[l].astype(BF16),
                    w_sh_down[l].astype(BF16))
    return x2.reshape(b, s, d)
```

```python
import numpy as np
import jax
import jax.numpy as jnp
from jax import lax
from jax.experimental import pallas as pl
from jax.experimental.pallas import tpu as pltpu

D_MODEL = 2048
SEQ = 2048
HEAD_DIM = 128
M_HEADS = 8
A_HEADS = 8
D_M = M_HEADS * HEAD_DIM
D_A = A_HEADS * HEAD_DIM
CONV_K = 4
MLSTM_CHUNK = 128
DILATED_PATTERNS = ((128, 1), (512, 4), (2048, 16))
REL_BUCKETS = 32
REL_MAX_DIST = 2048
N_EXPERTS = 256
TOP_K = 8
N_GROUPS = 8
TOPK_GROUPS = 4
D_EXPERT = 512
ROUTED_SCALE = 2.5
NORM_EPS = 1e-6
D_PROJ_MAIN = 4 * D_M + 3 * D_A
N_GATES = 2 * M_HEADS
LANES = 128
ATT_W = 128
MOE_TILE = 128
VMEM_LIMIT = 56 * 1024 * 1024
EXPERTS_VMEM_LIMIT = 60 * 1024 * 1024

F32 = jnp.float32
BF16 = jnp.bfloat16
NEG_INF = float("-inf")


def _cparams(sem):
    return pltpu.CompilerParams(dimension_semantics=sem, vmem_limit_bytes=VMEM_LIMIT)


def _dot(a, b):
    return jnp.dot(a, b, preferred_element_type=F32)


def _dot_nt(a, b):
    return lax.dot_general(a, b, (((1,), (1,)), ((), ())), preferred_element_type=F32)


def _dot_tn(a, b):
    return lax.dot_general(a, b, (((0,), (0,)), ((), ())), preferred_element_type=F32)


def _rms(x, g):
    ms = jnp.mean(x * x, axis=-1, keepdims=True)
    return x * lax.rsqrt(ms + NORM_EPS) * g


def _silu(x):
    return x * jax.nn.sigmoid(x)


def _ada_kernel(c_ref, w_ref, b_ref, o_ref):
    a = _silu(c_ref[...]).astype(BF16)
    o_ref[...] = _dot(a, w_ref[...].astype(BF16)) + b_ref[...]


def _ada(c_pad, w_ada, b_ada):
    tn = 1024
    n = w_ada.shape[1]
    return pl.pallas_call(
        _ada_kernel,
        grid=(n // tn,),
        in_specs=[pl.BlockSpec((8, D_MODEL), lambda j: (0, 0)),
                  pl.BlockSpec((D_MODEL, tn), lambda j: (0, j)),
                  pl.BlockSpec((1, tn), lambda j: (0, j))],
        out_specs=pl.BlockSpec((8, tn), lambda j: (0, j)),
        out_shape=jax.ShapeDtypeStruct((8, n), F32),
        compiler_params=_cparams(("arbitrary",)),
        name="ada_mod",
    )(c_pad, w_ada, b_ada)


def _inproj_kernel(x_ref, mod_ref, g_ref, w_ref, wg_ref, bif_ref, proj_ref, gates_ref, gates_t_ref, h_sc):
    @pl.when(pl.program_id(1) == 0)
    def _():
        h = _rms(x_ref[...], g_ref[...]) * (1.0 + mod_ref[1:2, :]) + mod_ref[0:1, :]
        hb = h.astype(BF16)
        h_sc[...] = hb
        g = _dot(hb, wg_ref[...].astype(BF16)) + bif_ref[...]
        lane = lax.broadcasted_iota(jnp.int32, g.shape, 1)
        logf = jnp.minimum(g, 0.0) - jnp.log1p(jnp.exp(-jnp.abs(g)))
        gates = jnp.where(lane < M_HEADS, g, logf)
        gates_ref[...] = gates
        gates_t_ref[...] = gates.T

    proj_ref[...] = _dot(h_sc[...], w_ref[...])


def _inproj(x2, mod3, g0, w_in_bf, w_gates, bif):
    tm, tn = 1024, 512
    t = x2.shape[0]
    return pl.pallas_call(
        _inproj_kernel,
        grid=(t // tm, D_PROJ_MAIN // tn),
        in_specs=[pl.BlockSpec((tm, D_MODEL), lambda i, j: (i, 0)),
                  pl.BlockSpec((None, 6, D_MODEL), lambda i, j: (i // (SEQ // tm), 0, 0)),
                  pl.BlockSpec((1, D_MODEL), lambda i, j: (0, 0)),
                  pl.BlockSpec((D_MODEL, tn), lambda i, j: (0, j)),
                  pl.BlockSpec((D_MODEL, LANES), lambda i, j: (0, 0)),
                  pl.BlockSpec((1, LANES), lambda i, j: (0, 0))],
        out_specs=[pl.BlockSpec((tm, tn), lambda i, j: (i, j)),
                   pl.BlockSpec((tm, LANES), lambda i, j: (i, 0)),
                   pl.BlockSpec((LANES, tm), lambda i, j: (0, i))],
        out_shape=[jax.ShapeDtypeStruct((t, D_PROJ_MAIN), F32),
                   jax.ShapeDtypeStruct((t, LANES), F32),
                   jax.ShapeDtypeStruct((LANES, t), F32)],
        scratch_shapes=[pltpu.VMEM((tm, D_MODEL), BF16)],
        compiler_params=_cparams(("arbitrary", "arbitrary")),
        name="in_proj",
    )(x2, mod3, g0, w_in_bf, w_gates, bif)


def _mlstm_kernel(q_ref, k_ref, v_ref, g_ref, gt_ref, cwq_ref, cwk_ref, cbq_ref, cbk_ref, o_ref,
                  ct_sc, n_sc, m_sc, pq_sc, pk_sc):
    lc = MLSTM_CHUNK

    @pl.when(pl.program_id(1) == 0)
    def _():
        ct_sc[...] = jnp.zeros_like(ct_sc)
        n_sc[...] = jnp.zeros_like(n_sc)
        m_sc[...] = jnp.zeros_like(m_sc)
        pq_sc[...] = jnp.zeros_like(pq_sc)
        pk_sc[...] = jnp.zeros_like(pk_sc)

    row_w = lax.broadcasted_iota(jnp.int32, (lc, D_M), 0)

    def conv(cur, prev, w_ref, b_ref):
        acc = cur * w_ref[CONV_K - 1:CONV_K, :] + b_ref[...]
        for s in range(1, CONV_K):
            sh = jnp.where(row_w >= s, pltpu.roll(cur, s, 0), pltpu.roll(prev, s, 0))
            acc = acc + sh * w_ref[CONV_K - 1 - s:CONV_K - s, :]
        return _silu(acc)

    q_raw = q_ref[...]
    k_raw = k_ref[...]
    q_all = conv(q_raw, pq_sc[...], cwq_ref, cbq_ref)
    k_all = conv(k_raw, pk_sc[...], cwk_ref, cbk_ref) * (HEAD_DIM ** -0.5)
    pq_sc[...] = q_raw
    pk_sc[...] = k_raw

    row = lax.broadcasted_iota(jnp.int32, (lc, lc), 0)
    col = lax.broadcasted_iota(jnp.int32, (lc, lc), 1)
    causal = col <= row
    tril = causal.astype(F32)
    triu = (row <= col).astype(F32)
    g = g_ref[...]
    gt = gt_ref[...]
    b_cols = jnp.dot(tril, g, precision=lax.Precision.HIGHEST, preferred_element_type=F32)
    b_rows = jnp.dot(gt, triu, precision=lax.Precision.HIGHEST, preferred_element_type=F32)

    for h in range(M_HEADS):
        sl = slice(h * HEAD_DIM, (h + 1) * HEAD_DIM)
        i_col = g[:, h:h + 1]
        b_col = b_cols[:, M_HEADS + h:M_HEADS + h + 1]
        i_row = gt[h:h + 1, :]
        b_row = b_rows[M_HEADS + h:M_HEADS + h + 1, :]
        m_prev = m_sc[h:h + 1, 0:1]
        n_prev = n_sc[h:h + 1, :]
        ct_prev = ct_sc[h]
        qh = q_all[:, sl]
        kh = k_all[:, sl]
        vh = v_ref[:, sl]
        qb = qh.astype(BF16)
        kb = kh.astype(BF16)

        dmat = jnp.where(causal, b_col - b_row + i_row, NEG_INF)
        inter = b_col + m_prev
        m_t = jnp.maximum(inter, jnp.max(dmat, axis=-1, keepdims=True))
        w_intra = jnp.exp(dmat - m_t)
        w_inter = jnp.exp(inter - m_t)
        s = _dot_nt(qb, kb) * w_intra
        num = _dot(s.astype(BF16), vh.astype(BF16)) + w_inter * _dot(qb, ct_prev.astype(BF16))
        den = jnp.sum(s, axis=-1, keepdims=True) + w_inter * jnp.sum(qh * n_prev, axis=-1, keepdims=True)
        o_ref[:, sl] = num / jnp.maximum(jnp.abs(den), jnp.exp(-m_t))

        b_last = b_col[lc - 1:lc, :]
        log_w = b_last - b_col + i_col
        m_new = jnp.maximum(b_last + m_prev, jnp.max(log_w, axis=0, keepdims=True))
        w_upd = jnp.exp(log_w - m_new)
        decay = jnp.exp(b_last + m_prev - m_new)
        ct_sc[h] = decay * ct_prev + _dot_tn(kb, (vh * w_upd).astype(BF16))
        n_sc[h:h + 1, :] = decay * n_prev + jnp.sum(kh * w_upd, axis=0, keepdims=True)
        m_sc[h:h + 1, :] = jnp.broadcast_to(m_new, (1, LANES))


def _mlstm(proj, gates, gates_t, conv_w, conv_b):
    lc = MLSTM_CHUNK
    nc = SEQ // lc
    t = proj.shape[0]
    return pl.pallas_call(
        _mlstm_kernel,
        grid=(t // SEQ, nc),
        in_specs=[pl.BlockSpec((lc, D_M), lambda b, c: (b * nc + c, 0)),
                  pl.BlockSpec((lc, D_M), lambda b, c: (b * nc + c, 1)),
                  pl.BlockSpec((lc, D_M), lambda b, c: (b * nc + c, 2)),
                  pl.BlockSpec((lc, LANES), lambda b, c: (b * nc + c, 0)),
                  pl.BlockSpec((N_GATES, lc), lambda b, c: (0, b * nc + c)),
                  pl.BlockSpec((CONV_K, D_M), lambda b, c: (0, 0)),
                  pl.BlockSpec((CONV_K, D_M), lambda b, c: (0, 1)),
                  pl.BlockSpec((1, D_M), lambda b, c: (0, 0)),
                  pl.BlockSpec((1, D_M), lambda b, c: (0, 1))],
        out_specs=pl.BlockSpec((lc, D_M), lambda b, c: (b * nc + c, 0)),
        out_shape=jax.ShapeDtypeStruct((t, D_M), F32),
        scratch_shapes=[pltpu.VMEM((M_HEADS, HEAD_DIM, HEAD_DIM), F32),
                        pltpu.VMEM((M_HEADS, HEAD_DIM), F32),
                        pltpu.VMEM((M_HEADS, LANES), F32),
                        pltpu.VMEM((lc, D_M), F32),
                        pltpu.VMEM((lc, D_M), F32)],
        compiler_params=_cparams(("arbitrary", "arbitrary")),
        name="mlstm",
    )(proj, proj, proj, gates, gates_t, conv_w, conv_w, conv_b, conv_b)


ATT_GROUP = 4
ATT_PAD = ATT_W * 4


ATT_HEADS_PER_STEP = 2


def _attn_kernel(*refs):
    nh = ATT_HEADS_PER_STEP
    ins, out_ref, scratch = refs[:4 * nh], refs[4 * nh], refs[4 * nh + 1:]
    per = len(scratch) // nh
    heads = [tuple(ins[4 * h:4 * h + 4]) + tuple(scratch[per * h:per * (h + 1)]) for h in range(nh)]
    w, g = ATT_W, ATT_GROUP
    gw = g * w
    row = lax.broadcasted_iota(jnp.int32, (w, w), 0)
    col = lax.broadcasted_iota(jnp.int32, (w, w), 1)
    cur_ok = col <= row
    prev_ok = col >= row
    scale = HEAD_DIM ** -0.5
    first_step = (pl.program_id(0) == 0) & (pl.program_id(1) == 0)

    def rows(start, size, d):
        return pl.ds(start, size) if d == 1 else pl.ds(start, size, stride=d)

    def with_ones(v):
        return jnp.concatenate([v, jnp.ones(v.shape, BF16)], axis=1)

    for (q_ref, k_ref, v_ref, bias_ref, o_sc, l_sc, bias_sc, kp_sc, vp_sc, s_sc, pa_sc, pb_sc) in heads:
        @pl.when(first_step)
        def _(kp_sc=kp_sc, vp_sc=vp_sc, pa_sc=pa_sc, pb_sc=pb_sc):
            kp_sc[0:ATT_PAD, :] = jnp.zeros((ATT_PAD, HEAD_DIM), F32)
            vp_sc[0:ATT_PAD, :] = jnp.zeros((ATT_PAD, HEAD_DIM), F32)
            pa_sc[...] = jnp.zeros_like(pa_sc)
            pb_sc[...] = jnp.zeros_like(pb_sc)

        kp_sc[ATT_PAD:, :] = k_ref[...]
        vp_sc[ATT_PAD:, :] = v_ref[...]
        for p in range(len(DILATED_PATTERNS)):
            bias_sc[p] = pltpu.roll(jnp.broadcast_to(bias_ref[p], (w, 2 * w)), 0, 1, stride=1, stride_axis=0)

    def banded_head(head, i, p, d, nb):
        q_ref, k_ref, v_ref, bias_ref, o_sc, l_sc, bias_sc, kp_sc, vp_sc, s_sc, pa_sc, pb_sc = head
        per_res = nb // g
        r = i // per_res
        n0 = (i % per_res) * g
        qstart = r + n0 * (w * d)
        kstart = ATT_PAD + qstart - w * d
        qb = q_ref[rows(qstart, gw, d), :].astype(BF16)
        kb = kp_sc[rows(kstart, gw + w, d), :].astype(BF16)
        vb = vp_sc[rows(kstart, gw + w, d), :].astype(BF16)
        s_sc[:, 0:gw + w] = _dot_nt(qb, kb)
        ms = []
        for j in range(g):
            sj = s_sc[j * w:(j + 1) * w, j * w:(j + 2) * w] * scale + bias_sc[p]
            ok = jnp.concatenate([prev_ok & ((n0 + j) > 0), cur_ok], axis=1)
            sj = jnp.where(ok, sj, NEG_INF)
            m = jnp.max(sj, axis=-1, keepdims=True)
            pa_sc[j * w:(j + 1) * w, j * w:(j + 2) * w] = jnp.exp(sj - m).astype(BF16)
            ms.append(m)
        acc = _dot(pa_sc[...], with_ones(vb))
        den = acc[:, HEAD_DIM:]
        o_sc[p, rows(qstart, gw, d), :] = acc[:, :HEAD_DIM] / den
        l_sc[p, rows(qstart, gw, d), :] = jnp.concatenate(ms, axis=0) + jnp.log(den)

    def diagonal_head(head, i, p, d):
        q_ref, k_ref, v_ref, bias_ref, o_sc, l_sc, bias_sc, kp_sc, vp_sc, s_sc, pa_sc, pb_sc = head
        starts = [i * g + j for j in range(g)]
        qb = jnp.concatenate([q_ref[rows(st, w, d), :] for st in starts], axis=0).astype(BF16)
        kb = jnp.concatenate([k_ref[rows(st, w, d), :] for st in starts], axis=0).astype(BF16)
        vb = jnp.concatenate([v_ref[rows(st, w, d), :] for st in starts], axis=0).astype(BF16)
        s_sc[:, 0:gw] = _dot_nt(qb, kb)
        ms = []
        for j in range(g):
            sj = s_sc[j * w:(j + 1) * w, j * w:(j + 1) * w] * scale + bias_sc[p, :, w:2 * w]
            sj = jnp.where(cur_ok, sj, NEG_INF)
            m = jnp.max(sj, axis=-1, keepdims=True)
            pb_sc[j * w:(j + 1) * w, j * w:(j + 1) * w] = jnp.exp(sj - m).astype(BF16)
            ms.append(m)
        acc = _dot(pb_sc[...], with_ones(vb))
        for j, st in enumerate(starts):
            den = acc[j * w:(j + 1) * w, HEAD_DIM:]
            o_sc[p, rows(st, w, d), :] = acc[j * w:(j + 1) * w, :HEAD_DIM] / den
            l_sc[p, rows(st, w, d), :] = ms[j] + jnp.log(den)

    for p, (_, d) in enumerate(DILATED_PATTERNS):
        nb = SEQ // d // w
        if nb >= g:
            def banded(i, carry, p=p, d=d, nb=nb):
                for head in heads:
                    banded_head(head, i, p, d, nb)
                return carry

            lax.fori_loop(0, d * nb // g, banded, 0)
        else:
            def diagonal(i, carry, p=p, d=d):
                for head in heads:
                    diagonal_head(head, i, p, d)
                return carry

            lax.fori_loop(0, d // g, diagonal, 0)

    chunk = 256

    def combine(c, carry):
        rs = pl.ds(pl.multiple_of(c * chunk, chunk), chunk)
        for h, head in enumerate(heads):
            o_sc, l_sc = head[4], head[5]
            l1, l2, l3 = l_sc[0, rs, :], l_sc[1, rs, :], l_sc[2, rs, :]
            lm = jnp.maximum(jnp.maximum(l1, l2), l3)
            e1, e2, e3 = jnp.exp(l1 - lm), jnp.exp(l2 - lm), jnp.exp(l3 - lm)
            tot = e1 + e2 + e3
            out_ref[rs, h * HEAD_DIM:(h + 1) * HEAD_DIM] = ((e1 / tot) * o_sc[0, rs, :] + (e2 / tot) * o_sc[1, rs, :]
                                                           + (e3 / tot) * o_sc[2, rs, :])
        return carry

    lax.fori_loop(0, SEQ // chunk, combine, 0)


def _attn(proj, bias_all):
    t = proj.shape[0]
    nh = ATT_HEADS_PER_STEP
    nblk = D_PROJ_MAIN // HEAD_DIM
    qi, ki, vi = nblk - 3 * A_HEADS, nblk - 2 * A_HEADS, nblk - A_HEADS
    npat = len(DILATED_PATTERNS)
    gw = ATT_GROUP * ATT_W

    def head(base, j):
        return pl.BlockSpec((SEQ, HEAD_DIM), lambda b, h: (b, base + h * nh + j))

    in_specs, args = [], []
    for j in range(nh):
        in_specs += [head(qi, j), head(ki, j), head(vi, j),
                     pl.BlockSpec((npat, None, 1, 2 * ATT_W), lambda b, h, j=j: (0, h * nh + j, 0, 0))]
        args += [proj, proj, proj, bias_all]
    per_head_scratch = [pltpu.VMEM((npat, SEQ, HEAD_DIM), F32),
                        pltpu.VMEM((npat, SEQ, HEAD_DIM), F32),
                        pltpu.VMEM((npat, ATT_W, 2 * ATT_W), F32),
                        pltpu.VMEM((ATT_PAD + SEQ, HEAD_DIM), F32),
                        pltpu.VMEM((ATT_PAD + SEQ, HEAD_DIM), F32),
                        pltpu.VMEM((gw, gw + ATT_W), F32),
                        pltpu.VMEM((gw, gw + ATT_W), BF16),
                        pltpu.VMEM((gw, gw), BF16)]
    return pl.pallas_call(
        _attn_kernel,
        grid=(t // SEQ, A_HEADS // nh),
        in_specs=in_specs,
        out_specs=pl.BlockSpec((SEQ, nh * HEAD_DIM), lambda b, h: (b, h)),
        out_shape=jax.ShapeDtypeStruct((t, D_A), F32),
        scratch_shapes=per_head_scratch * nh,
        compiler_params=_cparams(("arbitrary", "arbitrary")),
        name="dilated_attn",
    )(*args)


def _t5_causal_bucket(dist):
    max_exact = REL_BUCKETS // 2
    d = np.maximum(dist, 1).astype(np.float32)
    large = max_exact + (np.log(d / max_exact) / np.log(REL_MAX_DIST / max_exact)
                         * (REL_BUCKETS - max_exact)).astype(np.int32)
    return np.where(dist < max_exact, dist, np.minimum(large, REL_BUCKETS - 1)).astype(np.int32)


def _attn_bias(rel_bias):
    w = ATT_W
    j = np.clip(w - np.arange(2 * w), 0, w)
    tabs = [rel_bias[_t5_causal_bucket(j * dil)] for _, dil in DILATED_PATTERNS]
    return jnp.transpose(jnp.stack(tabs), (0, 2, 1))[:, :, None, :].astype(F32)


def _head_rms(x, g_ref, base):
    outs = []
    for h in range(x.shape[1] // HEAD_DIM):
        sl = slice(h * HEAD_DIM, (h + 1) * HEAD_DIM)
        outs.append(_rms(x[:, sl], g_ref[:, base + h * HEAD_DIM:base + (h + 1) * HEAD_DIM]))
    return outs


def _outproj_kernel(hm_ref, om_ref, ha_ref, x_ref, mod_ref, hg_ref, g_ref, w_ref, out_ref, cat_sc):
    gate = jax.nn.sigmoid(om_ref[...])
    for h, seg in enumerate(_head_rms(hm_ref[...], hg_ref, 0)):
        sl = slice(h * HEAD_DIM, (h + 1) * HEAD_DIM)
        cat_sc[:, sl] = (seg * gate[:, sl]).astype(BF16)
    for h, seg in enumerate(_head_rms(ha_ref[...], hg_ref, D_M)):
        cat_sc[:, D_M + h * HEAD_DIM:D_M + (h + 1) * HEAD_DIM] = seg.astype(BF16)
    y = _dot(cat_sc[...], w_ref[...])
    out_ref[...] = x_ref[...] + mod_ref[2:3, :] * _rms(y, g_ref[...])


def _outproj(hm, proj, ha, x2, mod3, head_gains, g1, w_out_bf):
    tm = 256
    t = x2.shape[0]
    half = lambda blk: pl.BlockSpec((tm, D_M), lambda i: (i, blk))
    full = pl.BlockSpec((tm, D_MODEL), lambda i: (i, 0))
    vec = pl.BlockSpec((1, D_MODEL), lambda i: (0, 0))
    return pl.pallas_call(
        _outproj_kernel,
        grid=(t // tm,),
        in_specs=[half(0), half(3), half(0), full,
                  pl.BlockSpec((None, 6, D_MODEL), lambda i: (i // (SEQ // tm), 0, 0)),
                  vec, vec,
                  pl.BlockSpec((D_MODEL, D_MODEL), lambda i: (0, 0))],
        out_specs=full,
        out_shape=jax.ShapeDtypeStruct((t, D_MODEL), F32),
        scratch_shapes=[pltpu.VMEM((tm, D_MODEL), BF16)],
        compiler_params=_cparams(("arbitrary",)),
        name="out_proj",
    )(hm, proj, ha, x2, mod3, head_gains, g1, w_out_bf)


def _router_kernel(x_ref, mod_ref, g_ref, w_ref, rb_ref, h_ref, hp_ref, idx_ref, gate_ref, cnt_ref):
    tm = x_ref.shape[0]
    h = _rms(x_ref[...], g_ref[...]) * (1.0 + mod_ref[4:5, :]) + mod_ref[3:4, :]
    h_ref[...] = h
    hb = h.astype(BF16)
    bits = lax.bitcast_convert_type(hb.astype(F32), jnp.uint32)
    hp_ref[...] = bits[:, D_MODEL // 2:] | lax.shift_right_logical(bits[:, :D_MODEL // 2], jnp.uint32(16))
    scores = jax.nn.sigmoid(_dot(hb, w_ref[...].astype(BF16))).T
    sel = scores + rb_ref[...]
    ge = N_EXPERTS // N_GROUPS
    grow = lax.broadcasted_iota(jnp.int32, (N_GROUPS, tm), 0)
    gscore = jnp.zeros((N_GROUPS, tm), F32)
    for g in range(N_GROUPS):
        blk = sel[g * ge:(g + 1) * ge, :]
        m1 = jnp.max(blk, axis=0, keepdims=True)
        is1 = blk == m1
        cnt = jnp.sum(is1.astype(F32), axis=0, keepdims=True)
        m2 = jnp.max(jnp.where(is1, NEG_INF, blk), axis=0, keepdims=True)
        gscore = jnp.where(grow == g, m1 + jnp.where(cnt >= 2.0, m1, m2), gscore)
    rank = jnp.zeros((N_GROUPS, tm), F32)
    for g in range(N_GROUPS):
        other = gscore[g:g + 1, :]
        beats = (other > gscore) | ((other == gscore) & (grow > g))
        rank = rank + beats.astype(F32)
    keep = rank < float(TOPK_GROUPS)
    xm = jnp.concatenate([jnp.where(keep[g:g + 1, :], sel[g * ge:(g + 1) * ge, :], NEG_INF)
                          for g in range(N_GROUPS)], axis=0)
    erow = lax.broadcasted_iota(jnp.int32, (N_EXPERTS, tm), 0).astype(F32)
    krow = lax.broadcasted_iota(jnp.int32, (TOP_K, tm), 0)
    idx = jnp.zeros((TOP_K, tm), F32)
    gk = jnp.zeros((TOP_K, tm), F32)
    chosen = jnp.zeros((N_EXPERTS, tm), F32)
    for k in range(TOP_K):
        m = jnp.max(xm, axis=0, keepdims=True)
        first = jnp.min(jnp.where(xm == m, erow, float(N_EXPERTS)), axis=0, keepdims=True)
        hit = erow == first
        idx = jnp.where(krow == k, first, idx)
        gk = jnp.where(krow == k, jnp.sum(jnp.where(hit, scores, 0.0), axis=0, keepdims=True), gk)
        chosen = chosen + hit.astype(F32)
        xm = jnp.where(hit, NEG_INF, xm)
    idx_ref[...] = idx.astype(jnp.int32)
    gate_ref[...] = gk / jnp.sum(gk, axis=0, keepdims=True) * ROUTED_SCALE

    @pl.when(pl.program_id(0) == 0)
    def _():
        cnt_ref[...] = jnp.zeros_like(cnt_ref)

    cnt_ref[...] += jnp.sum(chosen, axis=1, keepdims=True)


def _router(x1, mod3, g2, w_router, rb_col):
    tm = 256
    t = x1.shape[0]
    return pl.pallas_call(
        _router_kernel,
        grid=(t // tm,),
        in_specs=[pl.BlockSpec((tm, D_MODEL), lambda i: (i, 0)),
                  pl.BlockSpec((None, 6, D_MODEL), lambda i: (i // (SEQ // tm), 0, 0)),
                  pl.BlockSpec((1, D_MODEL), lambda i: (0, 0)),
                  pl.BlockSpec((D_MODEL, N_EXPERTS), lambda i: (0, 0)),
                  pl.BlockSpec((N_EXPERTS, 1), lambda i: (0, 0))],
        out_specs=[pl.BlockSpec((tm, D_MODEL), lambda i: (i, 0)),
                   pl.BlockSpec((tm, D_MODEL // 2), lambda i: (i, 0)),
                   pl.BlockSpec((TOP_K, tm), lambda i: (0, i)),
                   pl.BlockSpec((TOP_K, tm), lambda i: (0, i)),
                   pl.BlockSpec((N_EXPERTS, 1), lambda i: (0, 0))],
        out_shape=[jax.ShapeDtypeStruct((t, D_MODEL), F32),
                   jax.ShapeDtypeStruct((t, D_MODEL // 2), jnp.uint32),
                   jax.ShapeDtypeStruct((TOP_K, t), jnp.int32),
                   jax.ShapeDtypeStruct((TOP_K, t), F32),
                   jax.ShapeDtypeStruct((N_EXPERTS, 1), F32)],
        compiler_params=_cparams(("arbitrary",)),
        name="ffn_norm_router",
    )(x1, mod3, g2, w_router, rb_col)


CONVERT_ROWS_BYTES = 512 * 1024


def _experts_kernel(te_ref, first_ref, nxt_ref, valid_ref, off_ref, toka_ref, tokb_ref, hp_hbm, wg_hbm, wu_hbm, wd_hbm,
                    y_ref, hp_v, wg_f, wu_f, wd_f, wg_b, wu_b, wd_b, xbuf, hsem, wsem):
    tm = MOE_TILE
    i = pl.program_id(0)
    stages = ((wg_hbm, wg_f, wg_b), (wu_hbm, wu_f, wu_b), (wd_hbm, wd_f, wd_b))

    def w_copy(m, e):
        hbm, staging, _ = stages[m]
        return pltpu.make_async_copy(hbm.at[e], staging, wsem.at[m])

    @pl.when(i == 0)
    def _():
        resident = pltpu.make_async_copy(hp_hbm, hp_v, hsem)
        resident.start()
        for m in range(len(stages)):
            w_copy(m, te_ref[0]).start()
        resident.wait()

    @pl.when(first_ref[i] == 1)
    def _():
        has_next = nxt_ref[i] >= 0
        e_next = jnp.maximum(nxt_ref[i], 0)
        for m, (_, staging, cache) in enumerate(stages):
            w_copy(m, te_ref[i]).wait()
            rows = CONVERT_ROWS_BYTES // (staging.shape[1] * 4)
            for c in range(staging.shape[0] // rows):
                cache[c * rows:(c + 1) * rows, :] = staging[c * rows:(c + 1) * rows, :].astype(BF16)

            @pl.when(has_next)
            def _(m=m):
                w_copy(m, e_next).start()

    @pl.when(valid_ref[i] == 1)
    def _():
        base = off_ref[i] % LANES
        for r in range(tm):
            j = base + r
            tok = jnp.where(j < LANES, toka_ref[0, jnp.minimum(j, LANES - 1)], tokb_ref[0, jnp.maximum(j - LANES, 0)])
            xbuf[pl.ds(r, 1), :] = hp_v[pl.ds(tok, 1), :]
        packed = xbuf[...]
        lo = lax.bitcast_convert_type(lax.shift_left(packed, jnp.uint32(16)), F32).astype(BF16)
        hi = lax.bitcast_convert_type(packed & jnp.uint32(0xFFFF0000), F32).astype(BF16)
        x = jnp.concatenate([lo, hi], axis=1)
        hid = (_silu(_dot(x, wg_b[...])) * _dot(x, wu_b[...])).astype(BF16)
        y_ref[...] = _dot(hid, wd_b[...])

    @pl.when(valid_ref[i] == 0)
    def _():
        y_ref[...] = jnp.zeros_like(y_ref)


def _experts(sched, tok_rows, hp, w_gate, w_up, w_down):
    tm = MOE_TILE
    nt = sched[0].shape[0]
    any_spec = pl.BlockSpec(memory_space=pl.ANY)

    def tok_spec(ahead):
        return pl.BlockSpec((None, 1, LANES), lambda i, te, first, nxt, valid, off: (off[i] // LANES + ahead, 0, 0),
                            memory_space=pltpu.SMEM)

    grid_spec = pltpu.PrefetchScalarGridSpec(
        num_scalar_prefetch=len(sched),
        grid=(nt,),
        in_specs=[tok_spec(0), tok_spec(1), any_spec, any_spec, any_spec, any_spec],
        out_specs=pl.BlockSpec((tm, D_MODEL), lambda i, *_: (i, 0)),
        scratch_shapes=[pltpu.VMEM(hp.shape, jnp.uint32),
                        pltpu.VMEM((D_MODEL, D_EXPERT), F32),
                        pltpu.VMEM((D_MODEL, D_EXPERT), F32),
                        pltpu.VMEM((D_EXPERT, D_MODEL), F32),
                        pltpu.VMEM((D_MODEL, D_EXPERT), BF16),
                        pltpu.VMEM((D_MODEL, D_EXPERT), BF16),
                        pltpu.VMEM((D_EXPERT, D_MODEL), BF16),
                        pltpu.VMEM((tm, D_MODEL // 2), jnp.uint32),
                        pltpu.SemaphoreType.DMA(()),
                        pltpu.SemaphoreType.DMA((3,))],
    )
    return pl.pallas_call(
        _experts_kernel,
        grid_spec=grid_spec,
        out_shape=jax.ShapeDtypeStruct((nt * tm, D_MODEL), F32),
        compiler_params=pltpu.CompilerParams(dimension_semantics=("arbitrary",),
                                             vmem_limit_bytes=EXPERTS_VMEM_LIMIT),
        name="routed_experts",
    )(*sched, tok_rows, tok_rows, hp, w_gate, w_up, w_down)


def _final_kernel(pos_ref, posn_ref, x_ref, h_ref, gate_ref, mod_ref, g_ref, wg_ref, wu_ref, wd_ref, y_hbm, out_ref,
                  gbuf, gsem):
    tm = x_ref.shape[0]
    i = pl.program_id(0)
    last = pl.num_programs(0) - 1
    s = i % 2

    def g_copy(k, r, slot, row):
        return pltpu.make_async_copy(y_hbm.at[pl.ds(row, 1)], gbuf.at[slot, k, pl.ds(r, 1)], gsem.at[slot])

    def gather(slot, rows_ref):
        def per_k(k, carry):
            for r in range(tm):
                g_copy(k, r, slot, rows_ref[k, r]).start()
            return carry
        lax.fori_loop(0, TOP_K, per_k, 0)

    @pl.when(i == 0)
    def _():
        gather(0, pos_ref)

    @pl.when(i < last)
    def _():
        gather(1 - s, posn_ref)

    def wait_k(k, carry):
        for r in range(tm):
            g_copy(k, r, s, 0).wait()
        return carry
    lax.fori_loop(0, TOP_K, wait_k, 0)

    h = h_ref[...].astype(BF16)
    hid = (_silu(_dot(h, wg_ref[...])) * _dot(h, wu_ref[...])).astype(BF16)
    y = _dot(hid, wd_ref[...])
    gate = gate_ref[...]
    for k in range(TOP_K):
        y = y + gbuf[s, k] * gate[:, k:k + 1]
    out_ref[...] = x_ref[...] + mod_ref[5:6, :] * _rms(y, g_ref[...])


def _final(x1, h2, gates, y_rows, pos, mod3, g3, ws_gate, ws_up, ws_down):
    tm = 128
    t = x1.shape[0]
    nt = t // tm
    full = pl.BlockSpec((tm, D_MODEL), lambda i: (i, 0))

    def pos_spec(ahead):
        return pl.BlockSpec((None, TOP_K, tm), lambda i: (jnp.minimum(i + ahead, nt - 1), 0, 0),
                            memory_space=pltpu.SMEM)

    return pl.pallas_call(
        _final_kernel,
        grid=(nt,),
        in_specs=[pos_spec(0), pos_spec(1), full, full,
                  pl.BlockSpec((tm, TOP_K), lambda i: (i, 0)),
                  pl.BlockSpec((None, 6, D_MODEL), lambda i: (i // (SEQ // tm), 0, 0)),
                  pl.BlockSpec((1, D_MODEL), lambda i: (0, 0)),
                  pl.BlockSpec((D_MODEL, D_EXPERT), lambda i: (0, 0)),
                  pl.BlockSpec((D_MODEL, D_EXPERT), lambda i: (0, 0)),
                  pl.BlockSpec((D_EXPERT, D_MODEL), lambda i: (0, 0)),
                  pl.BlockSpec(memory_space=pl.ANY)],
        out_specs=full,
        out_shape=jax.ShapeDtypeStruct((t, D_MODEL), F32),
        scratch_shapes=[pltpu.VMEM((2, TOP_K, tm, D_MODEL), F32),
                        pltpu.SemaphoreType.DMA((2,))],
        compiler_params=_cparams(("arbitrary",)),
        name="shared_expert_final",
    )(pos, pos, x1, h2, gates, mod3, g3, ws_gate, ws_up, ws_down, y_rows)


def _dispatch(idx_t, counts):
    t = idx_t.shape[1]
    a = t * TOP_K
    tm = MOE_TILE
    nt = (a + N_EXPERTS * (tm - 1) + tm - 1) // tm
    e_flat = idx_t.reshape(-1)
    order = jnp.argsort(e_flat).astype(jnp.int32)
    rank = jnp.argsort(order).astype(jnp.int32)
    starts = jnp.cumsum(counts) - counts
    padded = (counts + tm - 1) // tm * tm
    pends = jnp.cumsum(padded)
    pstarts = pends - padded
    tile_start = jnp.arange(nt, dtype=jnp.int32) * tm
    te = jnp.minimum(jnp.sum(pends[None, :] <= tile_start[:, None], axis=1), N_EXPERTS - 1).astype(jnp.int32)
    valid = tile_start < pends[-1]
    off = jnp.where(valid, starts[te] + tile_start - pstarts[te], 0).astype(jnp.int32)
    shift = pstarts - starts
    experts = jnp.arange(N_EXPERTS, dtype=jnp.int32)
    pos = rank + jnp.sum(jnp.where(e_flat[:, None] == experts[None, :], shift[None, :], 0), axis=1)
    tiles = jnp.arange(nt, dtype=jnp.int32)
    first = jnp.concatenate([jnp.ones((1,), bool), te[1:] != te[:-1]])
    run_start = jnp.where(first, tiles, nt)
    next_start = jnp.concatenate([lax.cummin(run_start[::-1])[::-1][1:], jnp.full((1,), nt, jnp.int32)])
    nxt = jnp.where(next_start < nt, te[jnp.clip(next_start, 0, nt - 1)], -1).astype(jnp.int32)
    sched = (te, first.astype(jnp.int32), nxt, valid.astype(jnp.int32), off)
    tok_rows = jnp.pad((order % t).reshape(a // LANES, 1, LANES), ((0, 1), (0, 0), (0, 0)))
    pos_tiles = pos.astype(jnp.int32).reshape(TOP_K, t // tm, tm).transpose(1, 0, 2)
    return sched, tok_rows, pos_tiles


def kernel(x, c, w_ada, b_ada, norm_gains, w_in, b_if, conv_w, conv_b, head_gains, rel_bias, w_out,
           w_router, router_bias, w_exp_gate, w_exp_up, w_exp_down, w_sh_gate, w_sh_up, w_sh_down):
    b, s, d = x.shape
    t = b * s
    x2 = x.reshape(t, d)
    for l in range(w_ada.shape[0]):
        c_pad = jnp.pad(c, ((0, 8 - b), (0, 0)))
        mod = _ada(c_pad, w_ada[l], b_ada[l][None, :])[:b]
        mod3 = mod.reshape(b, 6, d)
        g = norm_gains[l]
        w_gates = jnp.pad(w_in[l][:, D_PROJ_MAIN:], ((0, 0), (0, LANES - N_GATES)))
        bif = jnp.pad(b_if[l][None, :], ((0, 0), (0, LANES - N_GATES)))
        proj, gates, gates_t = _inproj(x2, mod3, g[0][None, :], w_in[l].astype(BF16), w_gates, bif)
        hm = _mlstm(proj, gates, gates_t, conv_w[l], conv_b[l][None, :])
        ha = _attn(proj, _attn_bias(rel_bias))
        x1 = _outproj(hm, proj, ha, x2, mod3, head_gains[l][None, :], g[1][None, :], w_out[l].astype(BF16))
        h2, hp, idx_t, gate_t, cnt = _router(x1, mod3, g[2][None, :], w_router[l], router_bias[l][:, None])
        sched, tok_rows, pos = _dispatch(idx_t, cnt[:, 0].astype(jnp.int32))
        y_rows = _experts(sched, tok_rows, hp, w_exp_gate[l], w_exp_up[l], w_exp_down[l])
        x2 = _final(x1, h2, gate_t.T, y_rows, pos, mod3, g[3][None, :], w_sh_gate[l].astype(BF16), w_sh_up[l].astype(BF16),
                    w_sh_down[l].astype(BF16))
    return x2.reshape(b, s, d)
```

```python
import numpy as np
import jax
import jax.numpy as jnp
from jax import lax
from jax.experimental import pallas as pl
from jax.experimental.pallas import tpu as pltpu

D_MODEL = 2048
SEQ = 2048
HEAD_DIM = 128
M_HEADS = 8
A_HEADS = 8
D_M = M_HEADS * HEAD_DIM
D_A = A_HEADS * HEAD_DIM
CONV_K = 4
MLSTM_CHUNK = 128
DILATED_PATTERNS = ((128, 1), (512, 4), (2048, 16))
REL_BUCKETS = 32
REL_MAX_DIST = 2048
N_EXPERTS = 256
TOP_K = 8
N_GROUPS = 8
TOPK_GROUPS = 4
D_EXPERT = 512
ROUTED_SCALE = 2.5
NORM_EPS = 1e-6
D_PROJ_MAIN = 4 * D_M + 3 * D_A
N_GATES = 2 * M_HEADS
LANES = 128
ATT_W = 128
MOE_TILE = 128
VMEM_LIMIT = 56 * 1024 * 1024
EXPERTS_VMEM_LIMIT = 60 * 1024 * 1024

F32 = jnp.float32
BF16 = jnp.bfloat16
NEG_INF = float("-inf")


def _cparams(sem):
    return pltpu.CompilerParams(dimension_semantics=sem, vmem_limit_bytes=VMEM_LIMIT)


def _dot(a, b):
    return jnp.dot(a, b, preferred_element_type=F32)


def _dot_nt(a, b):
    return lax.dot_general(a, b, (((1,), (1,)), ((), ())), preferred_element_type=F32)


def _dot_tn(a, b):
    return lax.dot_general(a, b, (((0,), (0,)), ((), ())), preferred_element_type=F32)


def _rms(x, g):
    ms = jnp.mean(x * x, axis=-1, keepdims=True)
    return x * lax.rsqrt(ms + NORM_EPS) * g


def _silu(x):
    return x * jax.nn.sigmoid(x)


def _ada_kernel(c_ref, w_ref, b_ref, o_ref):
    a = _silu(c_ref[...]).astype(BF16)
    o_ref[...] = _dot(a, w_ref[...].astype(BF16)) + b_ref[...]


def _ada(c_pad, w_ada, b_ada):
    tn = 1024
    n = w_ada.shape[1]
    return pl.pallas_call(
        _ada_kernel,
        grid=(n // tn,),
        in_specs=[pl.BlockSpec((8, D_MODEL), lambda j: (0, 0)),
                  pl.BlockSpec((D_MODEL, tn), lambda j: (0, j)),
                  pl.BlockSpec((1, tn), lambda j: (0, j))],
        out_specs=pl.BlockSpec((8, tn), lambda j: (0, j)),
        out_shape=jax.ShapeDtypeStruct((8, n), F32),
        compiler_params=_cparams(("arbitrary",)),
        name="ada_mod",
    )(c_pad, w_ada, b_ada)


def _inproj_kernel(x_ref, mod_ref, g_ref, w_ref, wg_ref, bif_ref, proj_ref, gates_ref, gates_t_ref, h_sc):
    @pl.when(pl.program_id(1) == 0)
    def _():
        h = _rms(x_ref[...], g_ref[...]) * (1.0 + mod_ref[1:2, :]) + mod_ref[0:1, :]
        hb = h.astype(BF16)
        h_sc[...] = hb
        wlane = lax.broadcasted_iota(jnp.int32, wg_ref.shape, 1)
        wg = jnp.where(wlane < N_GATES, wg_ref[...], 0.0).astype(BF16)
        g = _dot(hb, wg) + bif_ref[...]
        lane = lax.broadcasted_iota(jnp.int32, g.shape, 1)
        logf = jnp.minimum(g, 0.0) - jnp.log1p(jnp.exp(-jnp.abs(g)))
        gates = jnp.where(lane < M_HEADS, g, logf)
        gates_ref[...] = gates
        gates_t_ref[...] = gates.T

    proj_ref[...] = _dot(h_sc[...], w_ref[...])


def _inproj(x2, mod3, g0, w_in_bf, w_in, bif):
    tm, tn = 1024, 1024
    t = x2.shape[0]
    return pl.pallas_call(
        _inproj_kernel,
        grid=(t // tm, D_PROJ_MAIN // tn),
        in_specs=[pl.BlockSpec((tm, D_MODEL), lambda i, j: (i, 0)),
                  pl.BlockSpec((None, 6, D_MODEL), lambda i, j: (i // (SEQ // tm), 0, 0)),
                  pl.BlockSpec((1, D_MODEL), lambda i, j: (0, 0)),
                  pl.BlockSpec((D_MODEL, tn), lambda i, j: (0, j)),
                  pl.BlockSpec((D_MODEL, LANES), lambda i, j: (0, D_PROJ_MAIN // LANES)),
                  pl.BlockSpec((1, LANES), lambda i, j: (0, 0))],
        out_specs=[pl.BlockSpec((tm, tn), lambda i, j: (i, j)),
                   pl.BlockSpec((tm, LANES), lambda i, j: (i, 0)),
                   pl.BlockSpec((LANES, tm), lambda i, j: (0, i))],
        out_shape=[jax.ShapeDtypeStruct((t, D_PROJ_MAIN), F32),
                   jax.ShapeDtypeStruct((t, LANES), F32),
                   jax.ShapeDtypeStruct((LANES, t), F32)],
        scratch_shapes=[pltpu.VMEM((tm, D_MODEL), BF16)],
        compiler_params=_cparams(("arbitrary", "arbitrary")),
        name="in_proj",
    )(x2, mod3, g0, w_in_bf, w_in, bif)


def _mlstm_kernel(q_ref, k_ref, v_ref, g_ref, gt_ref, cwq_ref, cwk_ref, cbq_ref, cbk_ref, o_ref,
                  ct_sc, n_sc, m_sc, pq_sc, pk_sc):
    lc = MLSTM_CHUNK

    @pl.when(pl.program_id(1) == 0)
    def _():
        ct_sc[...] = jnp.zeros_like(ct_sc)
        n_sc[...] = jnp.zeros_like(n_sc)
        m_sc[...] = jnp.zeros_like(m_sc)
        pq_sc[...] = jnp.zeros_like(pq_sc)
        pk_sc[...] = jnp.zeros_like(pk_sc)

    row_w = lax.broadcasted_iota(jnp.int32, (lc, D_M), 0)

    def conv(cur, prev, w_ref, b_ref):
        acc = cur * w_ref[CONV_K - 1:CONV_K, :] + b_ref[...]
        for s in range(1, CONV_K):
            sh = jnp.where(row_w >= s, pltpu.roll(cur, s, 0), pltpu.roll(prev, s, 0))
            acc = acc + sh * w_ref[CONV_K - 1 - s:CONV_K - s, :]
        return _silu(acc)

    q_raw = q_ref[...]
    k_raw = k_ref[...]
    q_all = conv(q_raw, pq_sc[...], cwq_ref, cbq_ref)
    k_all = conv(k_raw, pk_sc[...], cwk_ref, cbk_ref) * (HEAD_DIM ** -0.5)
    pq_sc[...] = q_raw
    pk_sc[...] = k_raw

    row = lax.broadcasted_iota(jnp.int32, (lc, lc), 0)
    col = lax.broadcasted_iota(jnp.int32, (lc, lc), 1)
    causal = col <= row
    tril = causal.astype(F32)
    triu = (row <= col).astype(F32)
    g = g_ref[...]
    gt = gt_ref[...]
    b_cols = jnp.dot(tril, g, precision=lax.Precision.HIGHEST, preferred_element_type=F32)
    b_rows = jnp.dot(gt, triu, precision=lax.Precision.HIGHEST, preferred_element_type=F32)

    for h in range(M_HEADS):
        sl = slice(h * HEAD_DIM, (h + 1) * HEAD_DIM)
        i_col = g[:, h:h + 1]
        b_col = b_cols[:, M_HEADS + h:M_HEADS + h + 1]
        i_row = gt[h:h + 1, :]
        b_row = b_rows[M_HEADS + h:M_HEADS + h + 1, :]
        m_prev = m_sc[h:h + 1, 0:1]
        n_prev = n_sc[h:h + 1, :]
        ct_prev = ct_sc[h]
        qh = q_all[:, sl]
        kh = k_all[:, sl]
        vh = v_ref[:, sl]
        qb = qh.astype(BF16)
        kb = kh.astype(BF16)

        dmat = jnp.where(causal, b_col - b_row + i_row, NEG_INF)
        inter = b_col + m_prev
        m_t = jnp.maximum(inter, jnp.max(dmat, axis=-1, keepdims=True))
        w_intra = jnp.exp(dmat - m_t)
        w_inter = jnp.exp(inter - m_t)
        s = _dot_nt(qb, kb) * w_intra
        num = _dot(s.astype(BF16), vh.astype(BF16)) + w_inter * _dot(qb, ct_prev.astype(BF16))
        den = jnp.sum(s, axis=-1, keepdims=True) + w_inter * jnp.sum(qh * n_prev, axis=-1, keepdims=True)
        o_ref[:, sl] = num / jnp.maximum(jnp.abs(den), jnp.exp(-m_t))

        b_last = b_col[lc - 1:lc, :]
        log_w = b_last - b_col + i_col
        m_new = jnp.maximum(b_last + m_prev, jnp.max(log_w, axis=0, keepdims=True))
        w_upd = jnp.exp(log_w - m_new)
        decay = jnp.exp(b_last + m_prev - m_new)
        ct_sc[h] = decay * ct_prev + _dot_tn(kb, (vh * w_upd).astype(BF16))
        n_sc[h:h + 1, :] = decay * n_prev + jnp.sum(kh * w_upd, axis=0, keepdims=True)
        m_sc[h:h + 1, :] = jnp.broadcast_to(m_new, (1, LANES))


def _mlstm(proj, gates, gates_t, conv_w, conv_b):
    lc = MLSTM_CHUNK
    nc = SEQ // lc
    t = proj.shape[0]
    return pl.pallas_call(
        _mlstm_kernel,
        grid=(t // SEQ, nc),
        in_specs=[pl.BlockSpec((lc, D_M), lambda b, c: (b * nc + c, 0)),
                  pl.BlockSpec((lc, D_M), lambda b, c: (b * nc + c, 1)),
                  pl.BlockSpec((lc, D_M), lambda b, c: (b * nc + c, 2)),
                  pl.BlockSpec((lc, LANES), lambda b, c: (b * nc + c, 0)),
                  pl.BlockSpec((N_GATES, lc), lambda b, c: (0, b * nc + c)),
                  pl.BlockSpec((CONV_K, D_M), lambda b, c: (0, 0)),
                  pl.BlockSpec((CONV_K, D_M), lambda b, c: (0, 1)),
                  pl.BlockSpec((1, D_M), lambda b, c: (0, 0)),
                  pl.BlockSpec((1, D_M), lambda b, c: (0, 1))],
        out_specs=pl.BlockSpec((lc, D_M), lambda b, c: (b * nc + c, 0)),
        out_shape=jax.ShapeDtypeStruct((t, D_M), F32),
        scratch_shapes=[pltpu.VMEM((M_HEADS, HEAD_DIM, HEAD_DIM), F32),
                        pltpu.VMEM((M_HEADS, HEAD_DIM), F32),
                        pltpu.VMEM((M_HEADS, LANES), F32),
                        pltpu.VMEM((lc, D_M), F32),
                        pltpu.VMEM((lc, D_M), F32)],
        compiler_params=_cparams(("arbitrary", "arbitrary")),
        name="mlstm",
    )(proj, proj, proj, gates, gates_t, conv_w, conv_w, conv_b, conv_b)


ATT_GROUP = 4
ATT_PAD = ATT_W * 4


ATT_HEADS_PER_STEP = 2


def _attn_kernel(*refs):
    nh = ATT_HEADS_PER_STEP
    ins, out_ref, scratch = refs[:4 * nh], refs[4 * nh], refs[4 * nh + 1:]
    per = len(scratch) // nh
    heads = [tuple(ins[4 * h:4 * h + 4]) + tuple(scratch[per * h:per * (h + 1)]) for h in range(nh)]
    w, g = ATT_W, ATT_GROUP
    gw = g * w
    row = lax.broadcasted_iota(jnp.int32, (w, w), 0)
    col = lax.broadcasted_iota(jnp.int32, (w, w), 1)
    cur_ok = col <= row
    prev_ok = col >= row
    scale = HEAD_DIM ** -0.5
    first_step = (pl.program_id(0) == 0) & (pl.program_id(1) == 0)

    def rows(start, size, d):
        return pl.ds(start, size) if d == 1 else pl.ds(start, size, stride=d)

    def with_ones(v):
        return jnp.concatenate([v, jnp.ones(v.shape, BF16)], axis=1)

    for (q_ref, k_ref, v_ref, bias_ref, o_sc, l_sc, bias_sc, kp_sc, vp_sc, s_sc, pa_sc, pb_sc) in heads:
        @pl.when(first_step)
        def _(kp_sc=kp_sc, vp_sc=vp_sc, pa_sc=pa_sc, pb_sc=pb_sc):
            kp_sc[0:ATT_PAD, :] = jnp.zeros((ATT_PAD, HEAD_DIM), F32)
            vp_sc[0:ATT_PAD, :] = jnp.zeros((ATT_PAD, HEAD_DIM), F32)
            pa_sc[...] = jnp.zeros_like(pa_sc)
            pb_sc[...] = jnp.zeros_like(pb_sc)

        kp_sc[ATT_PAD:, :] = k_ref[...]
        vp_sc[ATT_PAD:, :] = v_ref[...]
        for p in range(len(DILATED_PATTERNS)):
            bias_sc[p] = pltpu.roll(jnp.broadcast_to(bias_ref[p], (w, 2 * w)), 0, 1, stride=1, stride_axis=0)

    def banded_head(head, i, p, d, nb):
        q_ref, k_ref, v_ref, bias_ref, o_sc, l_sc, bias_sc, kp_sc, vp_sc, s_sc, pa_sc, pb_sc = head
        per_res = nb // g
        r = i // per_res
        n0 = (i % per_res) * g
        qstart = r + n0 * (w * d)
        kstart = ATT_PAD + qstart - w * d
        qb = q_ref[rows(qstart, gw, d), :].astype(BF16)
        kb = kp_sc[rows(kstart, gw + w, d), :].astype(BF16)
        vb = vp_sc[rows(kstart, gw + w, d), :].astype(BF16)
        s_sc[:, 0:gw + w] = _dot_nt(qb, kb)
        ms = []
        for j in range(g):
            sj = s_sc[j * w:(j + 1) * w, j * w:(j + 2) * w] * scale + bias_sc[p]
            ok = jnp.concatenate([prev_ok & ((n0 + j) > 0), cur_ok], axis=1)
            sj = jnp.where(ok, sj, NEG_INF)
            m = jnp.max(sj, axis=-1, keepdims=True)
            pa_sc[j * w:(j + 1) * w, j * w:(j + 2) * w] = jnp.exp(sj - m).astype(BF16)
            ms.append(m)
        acc = _dot(pa_sc[...], with_ones(vb))
        den = acc[:, HEAD_DIM:]
        o_sc[p, rows(qstart, gw, d), :] = acc[:, :HEAD_DIM] / den
        l_sc[p, rows(qstart, gw, d), :] = jnp.concatenate(ms, axis=0) + jnp.log(den)

    def diagonal_head(head, i, p, d):
        q_ref, k_ref, v_ref, bias_ref, o_sc, l_sc, bias_sc, kp_sc, vp_sc, s_sc, pa_sc, pb_sc = head
        starts = [i * g + j for j in range(g)]
        qb = jnp.concatenate([q_ref[rows(st, w, d), :] for st in starts], axis=0).astype(BF16)
        kb = jnp.concatenate([k_ref[rows(st, w, d), :] for st in starts], axis=0).astype(BF16)
        vb = jnp.concatenate([v_ref[rows(st, w, d), :] for st in starts], axis=0).astype(BF16)
        s_sc[:, 0:gw] = _dot_nt(qb, kb)
        ms = []
        for j in range(g):
            sj = s_sc[j * w:(j + 1) * w, j * w:(j + 1) * w] * scale + bias_sc[p, :, w:2 * w]
            sj = jnp.where(cur_ok, sj, NEG_INF)
            m = jnp.max(sj, axis=-1, keepdims=True)
            pb_sc[j * w:(j + 1) * w, j * w:(j + 1) * w] = jnp.exp(sj - m).astype(BF16)
            ms.append(m)
        acc = _dot(pb_sc[...], with_ones(vb))
        for j, st in enumerate(starts):
            den = acc[j * w:(j + 1) * w, HEAD_DIM:]
            o_sc[p, rows(st, w, d), :] = acc[j * w:(j + 1) * w, :HEAD_DIM] / den
            l_sc[p, rows(st, w, d), :] = ms[j] + jnp.log(den)

    for p, (_, d) in enumerate(DILATED_PATTERNS):
        nb = SEQ // d // w
        if nb >= g:
            def banded(i, carry, p=p, d=d, nb=nb):
                for head in heads:
                    banded_head(head, i, p, d, nb)
                return carry

            lax.fori_loop(0, d * nb // g, banded, 0)
        else:
            def diagonal(i, carry, p=p, d=d):
                for head in heads:
                    diagonal_head(head, i, p, d)
                return carry

            lax.fori_loop(0, d // g, diagonal, 0)

    chunk = 256

    def combine(c, carry):
        rs = pl.ds(pl.multiple_of(c * chunk, chunk), chunk)
        for h, head in enumerate(heads):
            o_sc, l_sc = head[4], head[5]
            l1, l2, l3 = l_sc[0, rs, :], l_sc[1, rs, :], l_sc[2, rs, :]
            lm = jnp.maximum(jnp.maximum(l1, l2), l3)
            e1, e2, e3 = jnp.exp(l1 - lm), jnp.exp(l2 - lm), jnp.exp(l3 - lm)
            tot = e1 + e2 + e3
            out_ref[rs, h * HEAD_DIM:(h + 1) * HEAD_DIM] = ((e1 / tot) * o_sc[0, rs, :] + (e2 / tot) * o_sc[1, rs, :]
                                                           + (e3 / tot) * o_sc[2, rs, :])
        return carry

    lax.fori_loop(0, SEQ // chunk, combine, 0)


def _attn(proj, bias_all):
    t = proj.shape[0]
    nh = ATT_HEADS_PER_STEP
    nblk = D_PROJ_MAIN // HEAD_DIM
    qi, ki, vi = nblk - 3 * A_HEADS, nblk - 2 * A_HEADS, nblk - A_HEADS
    npat = len(DILATED_PATTERNS)
    gw = ATT_GROUP * ATT_W

    def head(base, j):
        return pl.BlockSpec((SEQ, HEAD_DIM), lambda b, h: (b, base + h * nh + j))

    in_specs, args = [], []
    for j in range(nh):
        in_specs += [head(qi, j), head(ki, j), head(vi, j),
                     pl.BlockSpec((npat, None, 1, 2 * ATT_W), lambda b, h, j=j: (0, h * nh + j, 0, 0))]
        args += [proj, proj, proj, bias_all]
    per_head_scratch = [pltpu.VMEM((npat, SEQ, HEAD_DIM), F32),
                        pltpu.VMEM((npat, SEQ, HEAD_DIM), F32),
                        pltpu.VMEM((npat, ATT_W, 2 * ATT_W), F32),
                        pltpu.VMEM((ATT_PAD + SEQ, HEAD_DIM), F32),
                        pltpu.VMEM((ATT_PAD + SEQ, HEAD_DIM), F32),
                        pltpu.VMEM((gw, gw + ATT_W), F32),
                        pltpu.VMEM((gw, gw + ATT_W), BF16),
                        pltpu.VMEM((gw, gw), BF16)]
    return pl.pallas_call(
        _attn_kernel,
        grid=(t // SEQ, A_HEADS // nh),
        in_specs=in_specs,
        out_specs=pl.BlockSpec((SEQ, nh * HEAD_DIM), lambda b, h: (b, h)),
        out_shape=jax.ShapeDtypeStruct((t, D_A), F32),
        scratch_shapes=per_head_scratch * nh,
        compiler_params=_cparams(("arbitrary", "arbitrary")),
        name="dilated_attn",
    )(*args)


def _t5_causal_bucket(dist):
    max_exact = REL_BUCKETS // 2
    d = np.maximum(dist, 1).astype(np.float32)
    large = max_exact + (np.log(d / max_exact) / np.log(REL_MAX_DIST / max_exact)
                         * (REL_BUCKETS - max_exact)).astype(np.int32)
    return np.where(dist < max_exact, dist, np.minimum(large, REL_BUCKETS - 1)).astype(np.int32)


def _attn_bias(rel_bias):
    w = ATT_W
    j = np.clip(w - np.arange(2 * w), 0, w)
    tabs = [rel_bias[_t5_causal_bucket(j * dil)] for _, dil in DILATED_PATTERNS]
    return jnp.transpose(jnp.stack(tabs), (0, 2, 1))[:, :, None, :].astype(F32)


def _head_rms(x, g_ref, base):
    outs = []
    for h in range(x.shape[1] // HEAD_DIM):
        sl = slice(h * HEAD_DIM, (h + 1) * HEAD_DIM)
        outs.append(_rms(x[:, sl], g_ref[:, base + h * HEAD_DIM:base + (h + 1) * HEAD_DIM]))
    return outs


def _outproj_kernel(hm_ref, om_ref, ha_ref, x_ref, mod_ref, hg_ref, g_ref, w_ref, out_ref, cat_sc):
    gate = jax.nn.sigmoid(om_ref[...])
    for h, seg in enumerate(_head_rms(hm_ref[...], hg_ref, 0)):
        sl = slice(h * HEAD_DIM, (h + 1) * HEAD_DIM)
        cat_sc[:, sl] = (seg * gate[:, sl]).astype(BF16)
    for h, seg in enumerate(_head_rms(ha_ref[...], hg_ref, D_M)):
        cat_sc[:, D_M + h * HEAD_DIM:D_M + (h + 1) * HEAD_DIM] = seg.astype(BF16)
    y = _dot(cat_sc[...], w_ref[...])
    out_ref[...] = x_ref[...] + mod_ref[2:3, :] * _rms(y, g_ref[...])


def _outproj(hm, proj, ha, x2, mod3, head_gains, g1, w_out_bf):
    tm = 256
    t = x2.shape[0]
    half = lambda blk: pl.BlockSpec((tm, D_M), lambda i: (i, blk))
    full = pl.BlockSpec((tm, D_MODEL), lambda i: (i, 0))
    vec = pl.BlockSpec((1, D_MODEL), lambda i: (0, 0))
    return pl.pallas_call(
        _outproj_kernel,
        grid=(t // tm,),
        in_specs=[half(0), half(3), half(0), full,
                  pl.BlockSpec((None, 6, D_MODEL), lambda i: (i // (SEQ // tm), 0, 0)),
                  vec, vec,
                  pl.BlockSpec((D_MODEL, D_MODEL), lambda i: (0, 0))],
        out_specs=full,
        out_shape=jax.ShapeDtypeStruct((t, D_MODEL), F32),
        scratch_shapes=[pltpu.VMEM((tm, D_MODEL), BF16)],
        compiler_params=_cparams(("arbitrary",)),
        name="out_proj",
    )(hm, proj, ha, x2, mod3, head_gains, g1, w_out_bf)


def _router_kernel(x_ref, mod_ref, g_ref, w_ref, rb_ref, h_ref, hp_ref, idx_ref, gate_ref, cnt_ref):
    tm = x_ref.shape[0]
    h = _rms(x_ref[...], g_ref[...]) * (1.0 + mod_ref[4:5, :]) + mod_ref[3:4, :]
    h_ref[...] = h
    hb = h.astype(BF16)
    bits = lax.bitcast_convert_type(hb.astype(F32), jnp.uint32)
    hp_ref[...] = bits[:, D_MODEL // 2:] | lax.shift_right_logical(bits[:, :D_MODEL // 2], jnp.uint32(16))
    scores = jax.nn.sigmoid(_dot(hb, w_ref[...].astype(BF16))).T
    sel = scores + rb_ref[...]
    ge = N_EXPERTS // N_GROUPS
    grow = lax.broadcasted_iota(jnp.int32, (N_GROUPS, tm), 0)
    gscore = jnp.zeros((N_GROUPS, tm), F32)
    for g in range(N_GROUPS):
        blk = sel[g * ge:(g + 1) * ge, :]
        m1 = jnp.max(blk, axis=0, keepdims=True)
        is1 = blk == m1
        cnt = jnp.sum(is1.astype(F32), axis=0, keepdims=True)
        m2 = jnp.max(jnp.where(is1, NEG_INF, blk), axis=0, keepdims=True)
        gscore = jnp.where(grow == g, m1 + jnp.where(cnt >= 2.0, m1, m2), gscore)
    rank = jnp.zeros((N_GROUPS, tm), F32)
    for g in range(N_GROUPS):
        other = gscore[g:g + 1, :]
        beats = (other > gscore) | ((other == gscore) & (grow > g))
        rank = rank + beats.astype(F32)
    keep = rank < float(TOPK_GROUPS)
    xm = jnp.concatenate([jnp.where(keep[g:g + 1, :], sel[g * ge:(g + 1) * ge, :], NEG_INF)
                          for g in range(N_GROUPS)], axis=0)
    erow = lax.broadcasted_iota(jnp.int32, (N_EXPERTS, tm), 0).astype(F32)
    krow = lax.broadcasted_iota(jnp.int32, (TOP_K, tm), 0)
    idx = jnp.zeros((TOP_K, tm), F32)
    gk = jnp.zeros((TOP_K, tm), F32)
    chosen = jnp.zeros((N_EXPERTS, tm), F32)
    for k in range(TOP_K):
        m = jnp.max(xm, axis=0, keepdims=True)
        first = jnp.min(jnp.where(xm == m, erow, float(N_EXPERTS)), axis=0, keepdims=True)
        hit = erow == first
        idx = jnp.where(krow == k, first, idx)
        gk = jnp.where(krow == k, jnp.sum(jnp.where(hit, scores, 0.0), axis=0, keepdims=True), gk)
        chosen = chosen + hit.astype(F32)
        xm = jnp.where(hit, NEG_INF, xm)
    idx_ref[...] = idx.astype(jnp.int32)
    gate_ref[...] = gk / jnp.sum(gk, axis=0, keepdims=True) * ROUTED_SCALE

    @pl.when(pl.program_id(0) == 0)
    def _():
        cnt_ref[...] = jnp.zeros_like(cnt_ref)

    cnt_ref[...] += jnp.sum(chosen, axis=1, keepdims=True)


def _router(x1, mod3, g2, w_router, rb_col):
    tm = 256
    t = x1.shape[0]
    return pl.pallas_call(
        _router_kernel,
        grid=(t // tm,),
        in_specs=[pl.BlockSpec((tm, D_MODEL), lambda i: (i, 0)),
                  pl.BlockSpec((None, 6, D_MODEL), lambda i: (i // (SEQ // tm), 0, 0)),
                  pl.BlockSpec((1, D_MODEL), lambda i: (0, 0)),
                  pl.BlockSpec((D_MODEL, N_EXPERTS), lambda i: (0, 0)),
                  pl.BlockSpec((N_EXPERTS, 1), lambda i: (0, 0))],
        out_specs=[pl.BlockSpec((tm, D_MODEL), lambda i: (i, 0)),
                   pl.BlockSpec((tm, D_MODEL // 2), lambda i: (i, 0)),
                   pl.BlockSpec((TOP_K, tm), lambda i: (0, i)),
                   pl.BlockSpec((TOP_K, tm), lambda i: (0, i)),
                   pl.BlockSpec((N_EXPERTS, 1), lambda i: (0, 0))],
        out_shape=[jax.ShapeDtypeStruct((t, D_MODEL), F32),
                   jax.ShapeDtypeStruct((t, D_MODEL // 2), jnp.uint32),
                   jax.ShapeDtypeStruct((TOP_K, t), jnp.int32),
                   jax.ShapeDtypeStruct((TOP_K, t), F32),
                   jax.ShapeDtypeStruct((N_EXPERTS, 1), F32)],
        compiler_params=_cparams(("arbitrary",)),
        name="ffn_norm_router",
    )(x1, mod3, g2, w_router, rb_col)


CONVERT_ROWS_BYTES = 512 * 1024


def _experts_kernel(te_ref, first_ref, nxt_ref, valid_ref, off_ref, toka_ref, tokb_ref, hp_hbm, wg_hbm, wu_hbm, wd_hbm,
                    y_ref, hp_v, wg_f, wu_f, wd_f, wg_b, wu_b, wd_b, xbuf, hsem, wsem):
    tm = MOE_TILE
    i = pl.program_id(0)
    stages = ((wg_hbm, wg_f, wg_b), (wu_hbm, wu_f, wu_b), (wd_hbm, wd_f, wd_b))

    def w_copy(m, e):
        hbm, staging, _ = stages[m]
        return pltpu.make_async_copy(hbm.at[e], staging, wsem.at[m])

    @pl.when(i == 0)
    def _():
        resident = pltpu.make_async_copy(hp_hbm, hp_v, hsem)
        resident.start()
        for m in range(len(stages)):
            w_copy(m, te_ref[0]).start()
        resident.wait()

    @pl.when(first_ref[i] == 1)
    def _():
        has_next = nxt_ref[i] >= 0
        e_next = jnp.maximum(nxt_ref[i], 0)
        for m, (_, staging, cache) in enumerate(stages):
            w_copy(m, te_ref[i]).wait()
            rows = CONVERT_ROWS_BYTES // (staging.shape[1] * 4)
            for c in range(staging.shape[0] // rows):
                cache[c * rows:(c + 1) * rows, :] = staging[c * rows:(c + 1) * rows, :].astype(BF16)

            @pl.when(has_next)
            def _(m=m):
                w_copy(m, e_next).start()

    @pl.when(valid_ref[i] == 1)
    def _():
        base = off_ref[i] % LANES
        for r in range(tm):
            j = base + r
            tok = jnp.where(j < LANES, toka_ref[0, jnp.minimum(j, LANES - 1)], tokb_ref[0, jnp.maximum(j - LANES, 0)])
            xbuf[pl.ds(r, 1), :] = hp_v[pl.ds(tok, 1), :]
        packed = xbuf[...]
        lo = lax.bitcast_convert_type(lax.shift_left(packed, jnp.uint32(16)), F32).astype(BF16)
        hi = lax.bitcast_convert_type(packed & jnp.uint32(0xFFFF0000), F32).astype(BF16)
        x = jnp.concatenate([lo, hi], axis=1)
        hid = (_silu(_dot(x, wg_b[...])) * _dot(x, wu_b[...])).astype(BF16)
        y_ref[...] = _dot(hid, wd_b[...])

    @pl.when(valid_ref[i] == 0)
    def _():
        y_ref[...] = jnp.zeros_like(y_ref)


def _experts(sched, tok_rows, hp, w_gate, w_up, w_down):
    tm = MOE_TILE
    nt = sched[0].shape[0]
    any_spec = pl.BlockSpec(memory_space=pl.ANY)

    def tok_spec(ahead):
        return pl.BlockSpec((None, 1, LANES), lambda i, te, first, nxt, valid, off: (off[i] // LANES + ahead, 0, 0),
                            memory_space=pltpu.SMEM)

    grid_spec = pltpu.PrefetchScalarGridSpec(
        num_scalar_prefetch=len(sched),
        grid=(nt,),
        in_specs=[tok_spec(0), tok_spec(1), any_spec, any_spec, any_spec, any_spec],
        out_specs=pl.BlockSpec((tm, D_MODEL), lambda i, *_: (i, 0)),
        scratch_shapes=[pltpu.VMEM(hp.shape, jnp.uint32),
                        pltpu.VMEM((D_MODEL, D_EXPERT), F32),
                        pltpu.VMEM((D_MODEL, D_EXPERT), F32),
                        pltpu.VMEM((D_EXPERT, D_MODEL), F32),
                        pltpu.VMEM((D_MODEL, D_EXPERT), BF16),
                        pltpu.VMEM((D_MODEL, D_EXPERT), BF16),
                        pltpu.VMEM((D_EXPERT, D_MODEL), BF16),
                        pltpu.VMEM((tm, D_MODEL // 2), jnp.uint32),
                        pltpu.SemaphoreType.DMA(()),
                        pltpu.SemaphoreType.DMA((3,))],
    )
    return pl.pallas_call(
        _experts_kernel,
        grid_spec=grid_spec,
        out_shape=jax.ShapeDtypeStruct((nt * tm, D_MODEL), F32),
        compiler_params=pltpu.CompilerParams(dimension_semantics=("arbitrary",),
                                             vmem_limit_bytes=EXPERTS_VMEM_LIMIT),
        name="routed_experts",
    )(*sched, tok_rows, tok_rows, hp, w_gate, w_up, w_down)


def _final_kernel(pos_ref, posn_ref, x_ref, h_ref, gate_ref, mod_ref, g_ref, wg_ref, wu_ref, wd_ref, y_hbm, out_ref,
                  gbuf, gsem):
    tm = x_ref.shape[0]
    i = pl.program_id(0)
    last = pl.num_programs(0) - 1
    s = i % 2

    def g_copy(k, r, slot, row):
        return pltpu.make_async_copy(y_hbm.at[pl.ds(row, 1)], gbuf.at[slot, k, pl.ds(r, 1)], gsem.at[slot])

    def gather(slot, rows_ref):
        def per_k(k, carry):
            for r in range(tm):
                g_copy(k, r, slot, rows_ref[k, r]).start()
            return carry
        lax.fori_loop(0, TOP_K, per_k, 0)

    @pl.when(i == 0)
    def _():
        gather(0, pos_ref)

    @pl.when(i < last)
    def _():
        gather(1 - s, posn_ref)

    def wait_k(k, carry):
        for r in range(tm):
            g_copy(k, r, s, 0).wait()
        return carry
    lax.fori_loop(0, TOP_K, wait_k, 0)

    h = h_ref[...].astype(BF16)
    hid = (_silu(_dot(h, wg_ref[...])) * _dot(h, wu_ref[...])).astype(BF16)
    y = _dot(hid, wd_ref[...])
    gate = gate_ref[...]
    for k in range(TOP_K):
        y = y + gbuf[s, k] * gate[:, k:k + 1]
    out_ref[...] = x_ref[...] + mod_ref[5:6, :] * _rms(y, g_ref[...])


def _final(x1, h2, gates, y_rows, pos, mod3, g3, ws_gate, ws_up, ws_down):
    tm = 128
    t = x1.shape[0]
    nt = t // tm
    full = pl.BlockSpec((tm, D_MODEL), lambda i: (i, 0))

    def pos_spec(ahead):
        return pl.BlockSpec((None, TOP_K, tm), lambda i: (jnp.minimum(i + ahead, nt - 1), 0, 0),
                            memory_space=pltpu.SMEM)

    return pl.pallas_call(
        _final_kernel,
        grid=(nt,),
        in_specs=[pos_spec(0), pos_spec(1), full, full,
                  pl.BlockSpec((tm, TOP_K), lambda i: (i, 0)),
                  pl.BlockSpec((None, 6, D_MODEL), lambda i: (i // (SEQ // tm), 0, 0)),
                  pl.BlockSpec((1, D_MODEL), lambda i: (0, 0)),
                  pl.BlockSpec((D_MODEL, D_EXPERT), lambda i: (0, 0)),
                  pl.BlockSpec((D_MODEL, D_EXPERT), lambda i: (0, 0)),
                  pl.BlockSpec((D_EXPERT, D_MODEL), lambda i: (0, 0)),
                  pl.BlockSpec(memory_space=pl.ANY)],
        out_specs=full,
        out_shape=jax.ShapeDtypeStruct((t, D_MODEL), F32),
        scratch_shapes=[pltpu.VMEM((2, TOP_K, tm, D_MODEL), F32),
                        pltpu.SemaphoreType.DMA((2,))],
        compiler_params=_cparams(("arbitrary",)),
        name="shared_expert_final",
    )(pos, pos, x1, h2, gates, mod3, g3, ws_gate, ws_up, ws_down, y_rows)


def _dispatch(idx_t, counts):
    t = idx_t.shape[1]
    a = t * TOP_K
    tm = MOE_TILE
    nt = (a + N_EXPERTS * (tm - 1) + tm - 1) // tm
    e_flat = idx_t.reshape(-1)
    order = jnp.argsort(e_flat).astype(jnp.int32)
    rank = jnp.argsort(order).astype(jnp.int32)
    starts = jnp.cumsum(counts) - counts
    padded = (counts + tm - 1) // tm * tm
    pends = jnp.cumsum(padded)
    pstarts = pends - padded
    tile_start = jnp.arange(nt, dtype=jnp.int32) * tm
    te = jnp.minimum(jnp.sum(pends[None, :] <= tile_start[:, None], axis=1), N_EXPERTS - 1).astype(jnp.int32)
    valid = tile_start < pends[-1]
    off = jnp.where(valid, starts[te] + tile_start - pstarts[te], 0).astype(jnp.int32)
    shift = pstarts - starts
    experts = jnp.arange(N_EXPERTS, dtype=jnp.int32)
    pos = rank + jnp.sum(jnp.where(e_flat[:, None] == experts[None, :], shift[None, :], 0), axis=1)
    tiles = jnp.arange(nt, dtype=jnp.int32)
    first = jnp.concatenate([jnp.ones((1,), bool), te[1:] != te[:-1]])
    run_start = jnp.where(first, tiles, nt)
    next_start = jnp.concatenate([lax.cummin(run_start[::-1])[::-1][1:], jnp.full((1,), nt, jnp.int32)])
    nxt = jnp.where(next_start < nt, te[jnp.clip(next_start, 0, nt - 1)], -1).astype(jnp.int32)
    sched = (te, first.astype(jnp.int32), nxt, valid.astype(jnp.int32), off)
    tok_rows = jnp.pad((order % t).reshape(a // LANES, 1, LANES), ((0, 1), (0, 0), (0, 0)))
    pos_tiles = pos.astype(jnp.int32).reshape(TOP_K, t // tm, tm).transpose(1, 0, 2)
    return sched, tok_rows, pos_tiles


def kernel(x, c, w_ada, b_ada, norm_gains, w_in, b_if, conv_w, conv_b, head_gains, rel_bias, w_out,
           w_router, router_bias, w_exp_gate, w_exp_up, w_exp_down, w_sh_gate, w_sh_up, w_sh_down):
    b, s, d = x.shape
    t = b * s
    x2 = x.reshape(t, d)
    for l in range(w_ada.shape[0]):
        c_pad = jnp.pad(c, ((0, 8 - b), (0, 0)))
        mod = _ada(c_pad, w_ada[l], b_ada[l][None, :])[:b]
        mod3 = mod.reshape(b, 6, d)
        g = norm_gains[l]
        bif = jnp.pad(b_if[l][None, :], ((0, 0), (0, LANES - N_GATES)))
        proj, gates, gates_t = _inproj(x2, mod3, g[0][None, :], w_in[l].astype(BF16), w_in[l], bif)
        hm = _mlstm(proj, gates, gates_t, conv_w[l], conv_b[l][None, :])
        ha = _attn(proj, _attn_bias(rel_bias))
        x1 = _outproj(hm, proj, ha, x2, mod3, head_gains[l][None, :], g[1][None, :], w_out[l].astype(BF16))
        h2, hp, idx_t, gate_t, cnt = _router(x1, mod3, g[2][None, :], w_router[l], router_bias[l][:, None])
        sched, tok_rows, pos = _dispatch(idx_t, cnt[:, 0].astype(jnp.int32))
        y_rows = _experts(sched, tok_rows, hp, w_exp_gate[l], w_exp_up[l], w_exp_down[l])
        x2 = _final(x1, h2, gate_t.T, y_rows, pos, mod3, g[3][None, :], w_sh_gate[l].astype(BF16), w_sh_up[l].astype(BF16),
                    w_sh_down[l].astype(BF16))
    return x2.reshape(b, s, d)
```

```python
import numpy as np
import jax
import jax.numpy as jnp
from jax import lax
from jax.experimental import pallas as pl
from jax.experimental.pallas import tpu as pltpu

D_MODEL = 2048
SEQ = 2048
HEAD_DIM = 128
M_HEADS = 8
A_HEADS = 8
D_M = M_HEADS * HEAD_DIM
D_A = A_HEADS * HEAD_DIM
CONV_K = 4
MLSTM_CHUNK = 128
DILATED_PATTERNS = ((128, 1), (512, 4), (2048, 16))
REL_BUCKETS = 32
REL_MAX_DIST = 2048
N_EXPERTS = 256
TOP_K = 8
N_GROUPS = 8
TOPK_GROUPS = 4
D_EXPERT = 512
ROUTED_SCALE = 2.5
NORM_EPS = 1e-6
D_PROJ_MAIN = 4 * D_M + 3 * D_A
N_GATES = 2 * M_HEADS
LANES = 128
ATT_W = 128
MOE_TILE = 128
VMEM_LIMIT = 56 * 1024 * 1024
EXPERTS_VMEM_LIMIT = 60 * 1024 * 1024

F32 = jnp.float32
BF16 = jnp.bfloat16
NEG_INF = float("-inf")


def _cparams(sem):
    return pltpu.CompilerParams(dimension_semantics=sem, vmem_limit_bytes=VMEM_LIMIT)


def _dot(a, b):
    return jnp.dot(a, b, preferred_element_type=F32)


def _dot_nt(a, b):
    return lax.dot_general(a, b, (((1,), (1,)), ((), ())), preferred_element_type=F32)


def _dot_tn(a, b):
    return lax.dot_general(a, b, (((0,), (0,)), ((), ())), preferred_element_type=F32)


def _rms(x, g):
    ms = jnp.mean(x * x, axis=-1, keepdims=True)
    return x * lax.rsqrt(ms + NORM_EPS) * g


def _silu(x):
    return x * jax.nn.sigmoid(x)


def _ada_kernel(c_ref, w_ref, b_ref, o_ref):
    a = _silu(c_ref[...]).astype(BF16)
    o_ref[...] = _dot(a, w_ref[...].astype(BF16)) + b_ref[...]


def _ada(c_pad, w_ada, b_ada):
    tn = 1024
    n = w_ada.shape[1]
    return pl.pallas_call(
        _ada_kernel,
        grid=(n // tn,),
        in_specs=[pl.BlockSpec((8, D_MODEL), lambda j: (0, 0)),
                  pl.BlockSpec((D_MODEL, tn), lambda j: (0, j)),
                  pl.BlockSpec((1, tn), lambda j: (0, j))],
        out_specs=pl.BlockSpec((8, tn), lambda j: (0, j)),
        out_shape=jax.ShapeDtypeStruct((8, n), F32),
        compiler_params=_cparams(("arbitrary",)),
        name="ada_mod",
    )(c_pad, w_ada, b_ada)


def _inproj_kernel(x_ref, mod_ref, g_ref, w_ref, wg_ref, bif_ref, proj_ref, gates_ref, gates_t_ref, h_sc):
    @pl.when(pl.program_id(1) == 0)
    def _():
        h = _rms(x_ref[...], g_ref[...]) * (1.0 + mod_ref[1:2, :]) + mod_ref[0:1, :]
        hb = h.astype(BF16)
        h_sc[...] = hb
        wlane = lax.broadcasted_iota(jnp.int32, wg_ref.shape, 1)
        wg = jnp.where(wlane < N_GATES, wg_ref[...], 0.0).astype(BF16)
        g = _dot(hb, wg) + bif_ref[...]
        lane = lax.broadcasted_iota(jnp.int32, g.shape, 1)
        logf = jnp.minimum(g, 0.0) - jnp.log1p(jnp.exp(-jnp.abs(g)))
        gates = jnp.where(lane < M_HEADS, g, logf)
        gates_ref[...] = gates
        gates_t_ref[...] = gates.T

    proj_ref[...] = _dot(h_sc[...], w_ref[...])


def _inproj(x2, mod3, g0, w_in_bf, w_in, bif):
    tm, tn = 1024, 1024
    t = x2.shape[0]
    return pl.pallas_call(
        _inproj_kernel,
        grid=(t // tm, D_PROJ_MAIN // tn),
        in_specs=[pl.BlockSpec((tm, D_MODEL), lambda i, j: (i, 0)),
                  pl.BlockSpec((None, 6, D_MODEL), lambda i, j: (i // (SEQ // tm), 0, 0)),
                  pl.BlockSpec((1, D_MODEL), lambda i, j: (0, 0)),
                  pl.BlockSpec((D_MODEL, tn), lambda i, j: (0, j)),
                  pl.BlockSpec((D_MODEL, LANES), lambda i, j: (0, D_PROJ_MAIN // LANES)),
                  pl.BlockSpec((1, LANES), lambda i, j: (0, 0))],
        out_specs=[pl.BlockSpec((tm, tn), lambda i, j: (i, j)),
                   pl.BlockSpec((tm, LANES), lambda i, j: (i, 0)),
                   pl.BlockSpec((LANES, tm), lambda i, j: (0, i))],
        out_shape=[jax.ShapeDtypeStruct((t, D_PROJ_MAIN), F32),
                   jax.ShapeDtypeStruct((t, LANES), F32),
                   jax.ShapeDtypeStruct((LANES, t), F32)],
        scratch_shapes=[pltpu.VMEM((tm, D_MODEL), BF16)],
        compiler_params=_cparams(("arbitrary", "arbitrary")),
        name="in_proj",
    )(x2, mod3, g0, w_in_bf, w_in, bif)


def _mlstm_kernel(q_ref, k_ref, v_ref, g_ref, gt_ref, cwq_ref, cwk_ref, cbq_ref, cbk_ref, o_ref,
                  ct_sc, n_sc, m_sc, pq_sc, pk_sc):
    lc = MLSTM_CHUNK

    @pl.when(pl.program_id(1) == 0)
    def _():
        ct_sc[...] = jnp.zeros_like(ct_sc)
        n_sc[...] = jnp.zeros_like(n_sc)
        m_sc[...] = jnp.zeros_like(m_sc)
        pq_sc[...] = jnp.zeros_like(pq_sc)
        pk_sc[...] = jnp.zeros_like(pk_sc)

    row_w = lax.broadcasted_iota(jnp.int32, (lc, D_M), 0)

    def conv(cur, prev, w_ref, b_ref):
        acc = cur * w_ref[CONV_K - 1:CONV_K, :] + b_ref[...]
        for s in range(1, CONV_K):
            sh = jnp.where(row_w >= s, pltpu.roll(cur, s, 0), pltpu.roll(prev, s, 0))
            acc = acc + sh * w_ref[CONV_K - 1 - s:CONV_K - s, :]
        return _silu(acc)

    q_raw = q_ref[...]
    k_raw = k_ref[...]
    q_all = conv(q_raw, pq_sc[...], cwq_ref, cbq_ref)
    k_all = conv(k_raw, pk_sc[...], cwk_ref, cbk_ref) * (HEAD_DIM ** -0.5)
    pq_sc[...] = q_raw
    pk_sc[...] = k_raw

    row = lax.broadcasted_iota(jnp.int32, (lc, lc), 0)
    col = lax.broadcasted_iota(jnp.int32, (lc, lc), 1)
    causal = col <= row
    tril = causal.astype(F32)
    triu = (row <= col).astype(F32)
    g = g_ref[...]
    gt = gt_ref[...]
    b_cols = jnp.dot(tril, g, precision=lax.Precision.HIGHEST, preferred_element_type=F32)
    b_rows = jnp.dot(gt, triu, precision=lax.Precision.HIGHEST, preferred_element_type=F32)

    for h in range(M_HEADS):
        sl = slice(h * HEAD_DIM, (h + 1) * HEAD_DIM)
        i_col = g[:, h:h + 1]
        b_col = b_cols[:, M_HEADS + h:M_HEADS + h + 1]
        i_row = gt[h:h + 1, :]
        b_row = b_rows[M_HEADS + h:M_HEADS + h + 1, :]
        m_prev = m_sc[h:h + 1, 0:1]
        n_prev = n_sc[h:h + 1, :]
        ct_prev = ct_sc[h]
        qh = q_all[:, sl]
        kh = k_all[:, sl]
        vh = v_ref[:, sl]
        qb = qh.astype(BF16)
        kb = kh.astype(BF16)

        dmat = jnp.where(causal, b_col - b_row + i_row, NEG_INF)
        inter = b_col + m_prev
        m_t = jnp.maximum(inter, jnp.max(dmat, axis=-1, keepdims=True))
        w_intra = jnp.exp(dmat - m_t)
        w_inter = jnp.exp(inter - m_t)
        s = _dot_nt(qb, kb) * w_intra
        num = _dot(s.astype(BF16), vh.astype(BF16)) + w_inter * _dot(qb, ct_prev.astype(BF16))
        den = jnp.sum(s, axis=-1, keepdims=True) + w_inter * jnp.sum(qh * n_prev, axis=-1, keepdims=True)
        o_ref[:, sl] = num / jnp.maximum(jnp.abs(den), jnp.exp(-m_t))

        b_last = b_col[lc - 1:lc, :]
        log_w = b_last - b_col + i_col
        m_new = jnp.maximum(b_last + m_prev, jnp.max(log_w, axis=0, keepdims=True))
        w_upd = jnp.exp(log_w - m_new)
        decay = jnp.exp(b_last + m_prev - m_new)
        ct_sc[h] = decay * ct_prev + _dot_tn(kb, (vh * w_upd).astype(BF16))
        n_sc[h:h + 1, :] = decay * n_prev + jnp.sum(kh * w_upd, axis=0, keepdims=True)
        m_sc[h:h + 1, :] = jnp.broadcast_to(m_new, (1, LANES))


def _mlstm(proj, gates, gates_t, conv_w, conv_b):
    lc = MLSTM_CHUNK
    nc = SEQ // lc
    t = proj.shape[0]
    return pl.pallas_call(
        _mlstm_kernel,
        grid=(t // SEQ, nc),
        in_specs=[pl.BlockSpec((lc, D_M), lambda b, c: (b * nc + c, 0)),
                  pl.BlockSpec((lc, D_M), lambda b, c: (b * nc + c, 1)),
                  pl.BlockSpec((lc, D_M), lambda b, c: (b * nc + c, 2)),
                  pl.BlockSpec((lc, LANES), lambda b, c: (b * nc + c, 0)),
                  pl.BlockSpec((N_GATES, lc), lambda b, c: (0, b * nc + c)),
                  pl.BlockSpec((CONV_K, D_M), lambda b, c: (0, 0)),
                  pl.BlockSpec((CONV_K, D_M), lambda b, c: (0, 1)),
                  pl.BlockSpec((1, D_M), lambda b, c: (0, 0)),
                  pl.BlockSpec((1, D_M), lambda b, c: (0, 1))],
        out_specs=pl.BlockSpec((lc, D_M), lambda b, c: (b * nc + c, 0)),
        out_shape=jax.ShapeDtypeStruct((t, D_M), F32),
        scratch_shapes=[pltpu.VMEM((M_HEADS, HEAD_DIM, HEAD_DIM), F32),
                        pltpu.VMEM((M_HEADS, HEAD_DIM), F32),
                        pltpu.VMEM((M_HEADS, LANES), F32),
                        pltpu.VMEM((lc, D_M), F32),
                        pltpu.VMEM((lc, D_M), F32)],
        compiler_params=_cparams(("arbitrary", "arbitrary")),
        name="mlstm",
    )(proj, proj, proj, gates, gates_t, conv_w, conv_w, conv_b, conv_b)


ATT_GROUP = 4
ATT_PAD = ATT_W * 4


ATT_HEADS_PER_STEP = 2


def _attn_kernel(*refs):
    nh = ATT_HEADS_PER_STEP
    ins, out_ref, scratch = refs[:4 * nh], refs[4 * nh], refs[4 * nh + 1:]
    per = len(scratch) // nh
    heads = [tuple(ins[4 * h:4 * h + 4]) + tuple(scratch[per * h:per * (h + 1)]) for h in range(nh)]
    w, g = ATT_W, ATT_GROUP
    gw = g * w
    row = lax.broadcasted_iota(jnp.int32, (w, w), 0)
    col = lax.broadcasted_iota(jnp.int32, (w, w), 1)
    cur_ok = col <= row
    prev_ok = col >= row
    scale = HEAD_DIM ** -0.5
    first_step = (pl.program_id(0) == 0) & (pl.program_id(1) == 0)

    def rows(start, size, d):
        return pl.ds(start, size) if d == 1 else pl.ds(start, size, stride=d)

    def with_ones(v):
        return jnp.concatenate([v, jnp.ones(v.shape, BF16)], axis=1)

    for (q_ref, k_ref, v_ref, bias_ref, o_sc, l_sc, bias_sc, kp_sc, vp_sc, s_sc, pa_sc, pb_sc) in heads:
        @pl.when(first_step)
        def _(kp_sc=kp_sc, vp_sc=vp_sc, pa_sc=pa_sc, pb_sc=pb_sc):
            kp_sc[0:ATT_PAD, :] = jnp.zeros((ATT_PAD, HEAD_DIM), F32)
            vp_sc[0:ATT_PAD, :] = jnp.zeros((ATT_PAD, HEAD_DIM), F32)
            pa_sc[...] = jnp.zeros_like(pa_sc)
            pb_sc[...] = jnp.zeros_like(pb_sc)

        kp_sc[ATT_PAD:, :] = k_ref[...]
        vp_sc[ATT_PAD:, :] = v_ref[...]
        for p in range(len(DILATED_PATTERNS)):
            bias_sc[p] = pltpu.roll(jnp.broadcast_to(bias_ref[p], (w, 2 * w)), 0, 1, stride=1, stride_axis=0)

    def banded_head(head, i, p, d, nb):
        q_ref, k_ref, v_ref, bias_ref, o_sc, l_sc, bias_sc, kp_sc, vp_sc, s_sc, pa_sc, pb_sc = head
        per_res = nb // g
        r = i // per_res
        n0 = (i % per_res) * g
        qstart = r + n0 * (w * d)
        kstart = ATT_PAD + qstart - w * d
        qb = q_ref[rows(qstart, gw, d), :].astype(BF16)
        kb = kp_sc[rows(kstart, gw + w, d), :].astype(BF16)
        vb = vp_sc[rows(kstart, gw + w, d), :].astype(BF16)
        s_sc[:, 0:gw + w] = _dot_nt(qb, kb)
        ms = []
        for j in range(g):
            sj = s_sc[j * w:(j + 1) * w, j * w:(j + 2) * w] * scale + bias_sc[p]
            ok = jnp.concatenate([prev_ok & ((n0 + j) > 0), cur_ok], axis=1)
            sj = jnp.where(ok, sj, NEG_INF)
            m = jnp.max(sj, axis=-1, keepdims=True)
            pa_sc[j * w:(j + 1) * w, j * w:(j + 2) * w] = jnp.exp(sj - m).astype(BF16)
            ms.append(m)
        acc = _dot(pa_sc[...], with_ones(vb))
        den = acc[:, HEAD_DIM:]
        o_sc[p, rows(qstart, gw, d), :] = acc[:, :HEAD_DIM] / den
        l_sc[p, rows(qstart, gw, d), :] = jnp.concatenate(ms, axis=0) + jnp.log(den)

    def diagonal_head(head, i, p, d):
        q_ref, k_ref, v_ref, bias_ref, o_sc, l_sc, bias_sc, kp_sc, vp_sc, s_sc, pa_sc, pb_sc = head
        starts = [i * g + j for j in range(g)]
        qb = jnp.concatenate([q_ref[rows(st, w, d), :] for st in starts], axis=0).astype(BF16)
        kb = jnp.concatenate([k_ref[rows(st, w, d), :] for st in starts], axis=0).astype(BF16)
        vb = jnp.concatenate([v_ref[rows(st, w, d), :] for st in starts], axis=0).astype(BF16)
        s_sc[:, 0:gw] = _dot_nt(qb, kb)
        ms = []
        for j in range(g):
            sj = s_sc[j * w:(j + 1) * w, j * w:(j + 1) * w] * scale + bias_sc[p, :, w:2 * w]
            sj = jnp.where(cur_ok, sj, NEG_INF)
            m = jnp.max(sj, axis=-1, keepdims=True)
            pb_sc[j * w:(j + 1) * w, j * w:(j + 1) * w] = jnp.exp(sj - m).astype(BF16)
            ms.append(m)
        acc = _dot(pb_sc[...], with_ones(vb))
        for j, st in enumerate(starts):
            den = acc[j * w:(j + 1) * w, HEAD_DIM:]
            o_sc[p, rows(st, w, d), :] = acc[j * w:(j + 1) * w, :HEAD_DIM] / den
            l_sc[p, rows(st, w, d), :] = ms[j] + jnp.log(den)

    for p, (_, d) in enumerate(DILATED_PATTERNS):
        nb = SEQ // d // w
        if nb >= g:
            def banded(i, carry, p=p, d=d, nb=nb):
                for head in heads:
                    banded_head(head, i, p, d, nb)
                return carry

            lax.fori_loop(0, d * nb // g, banded, 0)
        else:
            def diagonal(i, carry, p=p, d=d):
                for head in heads:
                    diagonal_head(head, i, p, d)
                return carry

            lax.fori_loop(0, d // g, diagonal, 0)

    chunk = 256

    def combine(c, carry):
        rs = pl.ds(pl.multiple_of(c * chunk, chunk), chunk)
        for h, head in enumerate(heads):
            o_sc, l_sc = head[4], head[5]
            l1, l2, l3 = l_sc[0, rs, :], l_sc[1, rs, :], l_sc[2, rs, :]
            lm = jnp.maximum(jnp.maximum(l1, l2), l3)
            e1, e2, e3 = jnp.exp(l1 - lm), jnp.exp(l2 - lm), jnp.exp(l3 - lm)
            tot = e1 + e2 + e3
            out_ref[rs, h * HEAD_DIM:(h + 1) * HEAD_DIM] = ((e1 / tot) * o_sc[0, rs, :] + (e2 / tot) * o_sc[1, rs, :]
                                                           + (e3 / tot) * o_sc[2, rs, :])
        return carry

    lax.fori_loop(0, SEQ // chunk, combine, 0)


def _attn(proj, bias_all):
    t = proj.shape[0]
    nh = ATT_HEADS_PER_STEP
    nblk = D_PROJ_MAIN // HEAD_DIM
    qi, ki, vi = nblk - 3 * A_HEADS, nblk - 2 * A_HEADS, nblk - A_HEADS
    npat = len(DILATED_PATTERNS)
    gw = ATT_GROUP * ATT_W

    def head(base, j):
        return pl.BlockSpec((SEQ, HEAD_DIM), lambda b, h: (b, base + h * nh + j))

    in_specs, args = [], []
    for j in range(nh):
        in_specs += [head(qi, j), head(ki, j), head(vi, j),
                     pl.BlockSpec((npat, None, 1, 2 * ATT_W), lambda b, h, j=j: (0, h * nh + j, 0, 0))]
        args += [proj, proj, proj, bias_all]
    per_head_scratch = [pltpu.VMEM((npat, SEQ, HEAD_DIM), F32),
                        pltpu.VMEM((npat, SEQ, HEAD_DIM), F32),
                        pltpu.VMEM((npat, ATT_W, 2 * ATT_W), F32),
                        pltpu.VMEM((ATT_PAD + SEQ, HEAD_DIM), F32),
                        pltpu.VMEM((ATT_PAD + SEQ, HEAD_DIM), F32),
                        pltpu.VMEM((gw, gw + ATT_W), F32),
                        pltpu.VMEM((gw, gw + ATT_W), BF16),
                        pltpu.VMEM((gw, gw), BF16)]
    return pl.pallas_call(
        _attn_kernel,
        grid=(t // SEQ, A_HEADS // nh),
        in_specs=in_specs,
        out_specs=pl.BlockSpec((SEQ, nh * HEAD_DIM), lambda b, h: (b, h)),
        out_shape=jax.ShapeDtypeStruct((t, D_A), F32),
        scratch_shapes=per_head_scratch * nh,
        compiler_params=_cparams(("arbitrary", "arbitrary")),
        name="dilated_attn",
    )(*args)


def _t5_causal_bucket(dist):
    max_exact = REL_BUCKETS // 2
    d = np.maximum(dist, 1).astype(np.float32)
    large = max_exact + (np.log(d / max_exact) / np.log(REL_MAX_DIST / max_exact)
                         * (REL_BUCKETS - max_exact)).astype(np.int32)
    return np.where(dist < max_exact, dist, np.minimum(large, REL_BUCKETS - 1)).astype(np.int32)


def _attn_bias(rel_bias):
    w = ATT_W
    j = np.clip(w - np.arange(2 * w), 0, w)
    tabs = [rel_bias[_t5_causal_bucket(j * dil)] for _, dil in DILATED_PATTERNS]
    return jnp.transpose(jnp.stack(tabs), (0, 2, 1))[:, :, None, :].astype(F32)


def _head_rms(x, g_ref, base):
    outs = []
    for h in range(x.shape[1] // HEAD_DIM):
        sl = slice(h * HEAD_DIM, (h + 1) * HEAD_DIM)
        outs.append(_rms(x[:, sl], g_ref[:, base + h * HEAD_DIM:base + (h + 1) * HEAD_DIM]))
    return outs


def _outproj_kernel(hm_ref, om_ref, ha_ref, x_ref, mod_ref, hg_ref, g_ref, w_ref, out_ref, cat_sc):
    gate = jax.nn.sigmoid(om_ref[...])
    for h, seg in enumerate(_head_rms(hm_ref[...], hg_ref, 0)):
        sl = slice(h * HEAD_DIM, (h + 1) * HEAD_DIM)
        cat_sc[:, sl] = (seg * gate[:, sl]).astype(BF16)
    for h, seg in enumerate(_head_rms(ha_ref[...], hg_ref, D_M)):
        cat_sc[:, D_M + h * HEAD_DIM:D_M + (h + 1) * HEAD_DIM] = seg.astype(BF16)
    y = _dot(cat_sc[...], w_ref[...])
    out_ref[...] = x_ref[...] + mod_ref[2:3, :] * _rms(y, g_ref[...])


def _outproj(hm, proj, ha, x2, mod3, head_gains, g1, w_out_bf):
    tm = 256
    t = x2.shape[0]
    half = lambda blk: pl.BlockSpec((tm, D_M), lambda i: (i, blk))
    full = pl.BlockSpec((tm, D_MODEL), lambda i: (i, 0))
    vec = pl.BlockSpec((1, D_MODEL), lambda i: (0, 0))
    return pl.pallas_call(
        _outproj_kernel,
        grid=(t // tm,),
        in_specs=[half(0), half(3), half(0), full,
                  pl.BlockSpec((None, 6, D_MODEL), lambda i: (i // (SEQ // tm), 0, 0)),
                  vec, vec,
                  pl.BlockSpec((D_MODEL, D_MODEL), lambda i: (0, 0))],
        out_specs=full,
        out_shape=jax.ShapeDtypeStruct((t, D_MODEL), F32),
        scratch_shapes=[pltpu.VMEM((tm, D_MODEL), BF16)],
        compiler_params=_cparams(("arbitrary",)),
        name="out_proj",
    )(hm, proj, ha, x2, mod3, head_gains, g1, w_out_bf)


def _router_kernel(x_ref, mod_ref, g_ref, w_ref, rb_ref, h_ref, hp_ref, idx_ref, gate_ref, cnt_ref):
    tm = x_ref.shape[0]
    h = _rms(x_ref[...], g_ref[...]) * (1.0 + mod_ref[4:5, :]) + mod_ref[3:4, :]
    h_ref[...] = h
    hb = h.astype(BF16)
    bits = lax.bitcast_convert_type(hb.astype(F32), jnp.uint32)
    hp_ref[...] = bits[:, D_MODEL // 2:] | lax.shift_right_logical(bits[:, :D_MODEL // 2], jnp.uint32(16))
    scores = jax.nn.sigmoid(_dot(hb, w_ref[...].astype(BF16))).T
    sel = scores + rb_ref[...]
    ge = N_EXPERTS // N_GROUPS
    grow = lax.broadcasted_iota(jnp.int32, (N_GROUPS, tm), 0)
    gscore = jnp.zeros((N_GROUPS, tm), F32)
    for g in range(N_GROUPS):
        blk = sel[g * ge:(g + 1) * ge, :]
        m1 = jnp.max(blk, axis=0, keepdims=True)
        is1 = blk == m1
        cnt = jnp.sum(is1.astype(F32), axis=0, keepdims=True)
        m2 = jnp.max(jnp.where(is1, NEG_INF, blk), axis=0, keepdims=True)
        gscore = jnp.where(grow == g, m1 + jnp.where(cnt >= 2.0, m1, m2), gscore)
    rank = jnp.zeros((N_GROUPS, tm), F32)
    for g in range(N_GROUPS):
        other = gscore[g:g + 1, :]
        beats = (other > gscore) | ((other == gscore) & (grow > g))
        rank = rank + beats.astype(F32)
    keep = rank < float(TOPK_GROUPS)
    xm = jnp.concatenate([jnp.where(keep[g:g + 1, :], sel[g * ge:(g + 1) * ge, :], NEG_INF)
                          for g in range(N_GROUPS)], axis=0)
    erow = lax.broadcasted_iota(jnp.int32, (N_EXPERTS, tm), 0).astype(F32)
    krow = lax.broadcasted_iota(jnp.int32, (TOP_K, tm), 0)
    idx = jnp.zeros((TOP_K, tm), F32)
    gk = jnp.zeros((TOP_K, tm), F32)
    chosen = jnp.zeros((N_EXPERTS, tm), F32)
    for k in range(TOP_K):
        m = jnp.max(xm, axis=0, keepdims=True)
        first = jnp.min(jnp.where(xm == m, erow, float(N_EXPERTS)), axis=0, keepdims=True)
        hit = erow == first
        idx = jnp.where(krow == k, first, idx)
        gk = jnp.where(krow == k, jnp.sum(jnp.where(hit, scores, 0.0), axis=0, keepdims=True), gk)
        chosen = chosen + hit.astype(F32)
        xm = jnp.where(hit, NEG_INF, xm)
    idx_ref[...] = idx.astype(jnp.int32)
    gate_ref[...] = gk / jnp.sum(gk, axis=0, keepdims=True) * ROUTED_SCALE

    @pl.when(pl.program_id(0) == 0)
    def _():
        cnt_ref[...] = jnp.zeros_like(cnt_ref)

    cnt_ref[...] += jnp.sum(chosen, axis=1, keepdims=True)


def _router(x1, mod3, g2, w_router, rb_col):
    tm = 256
    t = x1.shape[0]
    return pl.pallas_call(
        _router_kernel,
        grid=(t // tm,),
        in_specs=[pl.BlockSpec((tm, D_MODEL), lambda i: (i, 0)),
                  pl.BlockSpec((None, 6, D_MODEL), lambda i: (i // (SEQ // tm), 0, 0)),
                  pl.BlockSpec((1, D_MODEL), lambda i: (0, 0)),
                  pl.BlockSpec((D_MODEL, N_EXPERTS), lambda i: (0, 0)),
                  pl.BlockSpec((N_EXPERTS, 1), lambda i: (0, 0))],
        out_specs=[pl.BlockSpec((tm, D_MODEL), lambda i: (i, 0)),
                   pl.BlockSpec((tm, D_MODEL // 2), lambda i: (i, 0)),
                   pl.BlockSpec((TOP_K, tm), lambda i: (0, i)),
                   pl.BlockSpec((TOP_K, tm), lambda i: (0, i)),
                   pl.BlockSpec((N_EXPERTS, 1), lambda i: (0, 0))],
        out_shape=[jax.ShapeDtypeStruct((t, D_MODEL), F32),
                   jax.ShapeDtypeStruct((t, D_MODEL // 2), jnp.uint32),
                   jax.ShapeDtypeStruct((TOP_K, t), jnp.int32),
                   jax.ShapeDtypeStruct((TOP_K, t), F32),
                   jax.ShapeDtypeStruct((N_EXPERTS, 1), F32)],
        compiler_params=_cparams(("arbitrary",)),
        name="ffn_norm_router",
    )(x1, mod3, g2, w_router, rb_col)


CONVERT_ROWS_BYTES = 512 * 1024


def _experts_kernel(te_ref, first_ref, nxt_ref, valid_ref, off_ref, tok_ref, hp_hbm, wg_hbm, wu_hbm, wd_hbm,
                    y_ref, hp_v, wg_f, wu_f, wd_f, wg_b, wu_b, wd_b, xbuf, hsem, wsem):
    tm = MOE_TILE
    i = pl.program_id(0)
    stages = ((wg_hbm, wg_f, wg_b), (wu_hbm, wu_f, wu_b), (wd_hbm, wd_f, wd_b))

    def w_copy(m, e):
        hbm, staging, _ = stages[m]
        return pltpu.make_async_copy(hbm.at[e], staging, wsem.at[m])

    @pl.when(i == 0)
    def _():
        resident = pltpu.make_async_copy(hp_hbm, hp_v, hsem)
        resident.start()
        for m in range(len(stages)):
            w_copy(m, te_ref[0]).start()
        resident.wait()

    @pl.when(first_ref[i] == 1)
    def _():
        has_next = nxt_ref[i] >= 0
        e_next = jnp.maximum(nxt_ref[i], 0)
        for m, (_, staging, cache) in enumerate(stages):
            w_copy(m, te_ref[i]).wait()
            rows = CONVERT_ROWS_BYTES // (staging.shape[1] * 4)
            for c in range(staging.shape[0] // rows):
                cache[c * rows:(c + 1) * rows, :] = staging[c * rows:(c + 1) * rows, :].astype(BF16)

            @pl.when(has_next)
            def _(m=m):
                w_copy(m, e_next).start()

    @pl.when(valid_ref[i] == 1)
    def _():
        base = off_ref[i] % LANES
        for r in range(tm):
            xbuf[pl.ds(r, 1), :] = hp_v[pl.ds(tok_ref[0, base + r], 1), :]
        packed = xbuf[...]
        lo = lax.bitcast_convert_type(lax.shift_left(packed, jnp.uint32(16)), F32).astype(BF16)
        hi = lax.bitcast_convert_type(packed & jnp.uint32(0xFFFF0000), F32).astype(BF16)
        x = jnp.concatenate([lo, hi], axis=1)
        hid = (_silu(_dot(x, wg_b[...])) * _dot(x, wu_b[...])).astype(BF16)
        y_ref[...] = _dot(hid, wd_b[...])

    @pl.when(valid_ref[i] == 0)
    def _():
        y_ref[...] = jnp.zeros_like(y_ref)


def _experts(sched, tok_rows, hp, w_gate, w_up, w_down):
    tm = MOE_TILE
    nt = sched[0].shape[0]
    any_spec = pl.BlockSpec(memory_space=pl.ANY)
    tok_spec = pl.BlockSpec((None, 1, 2 * LANES), lambda i, te, first, nxt, valid, off: (off[i] // LANES, 0, 0),
                            memory_space=pltpu.SMEM)
    grid_spec = pltpu.PrefetchScalarGridSpec(
        num_scalar_prefetch=len(sched),
        grid=(nt,),
        in_specs=[tok_spec, any_spec, any_spec, any_spec, any_spec],
        out_specs=pl.BlockSpec((tm, D_MODEL), lambda i, *_: (i, 0)),
        scratch_shapes=[pltpu.VMEM(hp.shape, jnp.uint32),
                        pltpu.VMEM((D_MODEL, D_EXPERT), F32),
                        pltpu.VMEM((D_MODEL, D_EXPERT), F32),
                        pltpu.VMEM((D_EXPERT, D_MODEL), F32),
                        pltpu.VMEM((D_MODEL, D_EXPERT), BF16),
                        pltpu.VMEM((D_MODEL, D_EXPERT), BF16),
                        pltpu.VMEM((D_EXPERT, D_MODEL), BF16),
                        pltpu.VMEM((tm, D_MODEL // 2), jnp.uint32),
                        pltpu.SemaphoreType.DMA(()),
                        pltpu.SemaphoreType.DMA((3,))],
    )
    return pl.pallas_call(
        _experts_kernel,
        grid_spec=grid_spec,
        out_shape=jax.ShapeDtypeStruct((nt * tm, D_MODEL), F32),
        compiler_params=pltpu.CompilerParams(dimension_semantics=("arbitrary",),
                                             vmem_limit_bytes=EXPERTS_VMEM_LIMIT),
        name="routed_experts",
    )(*sched, tok_rows, hp, w_gate, w_up, w_down)


def _final_kernel(pos_ref, posn_ref, x_ref, h_ref, gate_ref, mod_ref, g_ref, wg_ref, wu_ref, wd_ref, y_hbm, out_ref,
                  gbuf, gsem):
    tm = x_ref.shape[0]
    i = pl.program_id(0)
    last = pl.num_programs(0) - 1
    s = i % 2

    def g_copy(k, r, slot, row):
        return pltpu.make_async_copy(y_hbm.at[pl.ds(row, 1)], gbuf.at[slot, k, pl.ds(r, 1)], gsem.at[slot])

    def gather(slot, rows_ref):
        def per_k(k, carry):
            for r in range(tm):
                g_copy(k, r, slot, rows_ref[k, r]).start()
            return carry
        lax.fori_loop(0, TOP_K, per_k, 0)

    @pl.when(i == 0)
    def _():
        gather(0, pos_ref)

    @pl.when(i < last)
    def _():
        gather(1 - s, posn_ref)

    def wait_k(k, carry):
        for r in range(tm):
            g_copy(k, r, s, 0).wait()
        return carry
    lax.fori_loop(0, TOP_K, wait_k, 0)

    h = h_ref[...].astype(BF16)
    hid = (_silu(_dot(h, wg_ref[...])) * _dot(h, wu_ref[...])).astype(BF16)
    y = _dot(hid, wd_ref[...])
    gate = gate_ref[...]
    for k in range(TOP_K):
        y = y + gbuf[s, k] * gate[:, k:k + 1]
    out_ref[...] = x_ref[...] + mod_ref[5:6, :] * _rms(y, g_ref[...])


def _final(x1, h2, gates, y_rows, pos, mod3, g3, ws_gate, ws_up, ws_down):
    tm = 128
    t = x1.shape[0]
    nt = t // tm
    full = pl.BlockSpec((tm, D_MODEL), lambda i: (i, 0))

    def pos_spec(ahead):
        return pl.BlockSpec((None, TOP_K, tm), lambda i: (jnp.minimum(i + ahead, nt - 1), 0, 0),
                            memory_space=pltpu.SMEM)

    return pl.pallas_call(
        _final_kernel,
        grid=(nt,),
        in_specs=[pos_spec(0), pos_spec(1), full, full,
                  pl.BlockSpec((tm, TOP_K), lambda i: (i, 0)),
                  pl.BlockSpec((None, 6, D_MODEL), lambda i: (i // (SEQ // tm), 0, 0)),
                  pl.BlockSpec((1, D_MODEL), lambda i: (0, 0)),
                  pl.BlockSpec((D_MODEL, D_EXPERT), lambda i: (0, 0)),
                  pl.BlockSpec((D_MODEL, D_EXPERT), lambda i: (0, 0)),
                  pl.BlockSpec((D_EXPERT, D_MODEL), lambda i: (0, 0)),
                  pl.BlockSpec(memory_space=pl.ANY)],
        out_specs=full,
        out_shape=jax.ShapeDtypeStruct((t, D_MODEL), F32),
        scratch_shapes=[pltpu.VMEM((2, TOP_K, tm, D_MODEL), F32),
                        pltpu.SemaphoreType.DMA((2,))],
        compiler_params=_cparams(("arbitrary",)),
        name="shared_expert_final",
    )(pos, pos, x1, h2, gates, mod3, g3, ws_gate, ws_up, ws_down, y_rows)


def _dispatch(idx_t, counts):
    t = idx_t.shape[1]
    a = t * TOP_K
    tm = MOE_TILE
    nt = (a + N_EXPERTS * (tm - 1) + tm - 1) // tm
    e_flat = idx_t.reshape(-1)
    order = jnp.argsort(e_flat).astype(jnp.int32)
    rank = jnp.argsort(order).astype(jnp.int32)
    starts = jnp.cumsum(counts) - counts
    padded = (counts + tm - 1) // tm * tm
    pends = jnp.cumsum(padded)
    pstarts = pends - padded
    tile_start = jnp.arange(nt, dtype=jnp.int32) * tm
    te = jnp.minimum(jnp.sum(pends[None, :] <= tile_start[:, None], axis=1), N_EXPERTS - 1).astype(jnp.int32)
    valid = tile_start < pends[-1]
    off = jnp.where(valid, starts[te] + tile_start - pstarts[te], 0).astype(jnp.int32)
    shift = pstarts - starts
    experts = jnp.arange(N_EXPERTS, dtype=jnp.int32)
    pos = rank + jnp.sum(jnp.where(e_flat[:, None] == experts[None, :], shift[None, :], 0), axis=1)
    tiles = jnp.arange(nt, dtype=jnp.int32)
    first = jnp.concatenate([jnp.ones((1,), bool), te[1:] != te[:-1]])
    run_start = jnp.where(first, tiles, nt)
    next_start = jnp.concatenate([lax.cummin(run_start[::-1])[::-1][1:], jnp.full((1,), nt, jnp.int32)])
    nxt = jnp.where(next_start < nt, te[jnp.clip(next_start, 0, nt - 1)], -1).astype(jnp.int32)
    sched = (te, first.astype(jnp.int32), nxt, valid.astype(jnp.int32), off)
    tok_sorted = (order % t).reshape(a // LANES, LANES)
    tok_next = jnp.concatenate([tok_sorted[1:], jnp.zeros((1, LANES), jnp.int32)], axis=0)
    tok_rows = jnp.concatenate([tok_sorted, tok_next], axis=1)[:, None, :]
    pos_tiles = pos.astype(jnp.int32).reshape(TOP_K, t // tm, tm).transpose(1, 0, 2)
    return sched, tok_rows, pos_tiles


def kernel(x, c, w_ada, b_ada, norm_gains, w_in, b_if, conv_w, conv_b, head_gains, rel_bias, w_out,
           w_router, router_bias, w_exp_gate, w_exp_up, w_exp_down, w_sh_gate, w_sh_up, w_sh_down):
    b, s, d = x.shape
    t = b * s
    x2 = x.reshape(t, d)
    for l in range(w_ada.shape[0]):
        c_pad = jnp.pad(c, ((0, 8 - b), (0, 0)))
        mod = _ada(c_pad, w_ada[l], b_ada[l][None, :])[:b]
        mod3 = mod.reshape(b, 6, d)
        g = norm_gains[l]
        bif = jnp.pad(b_if[l][None, :], ((0, 0), (0, LANES - N_GATES)))
        proj, gates, gates_t = _inproj(x2, mod3, g[0][None, :], w_in[l].astype(BF16), w_in[l], bif)
        hm = _mlstm(proj, gates, gates_t, conv_w[l], conv_b[l][None, :])
        ha = _attn(proj, _attn_bias(rel_bias))
        x1 = _outproj(hm, proj, ha, x2, mod3, head_gains[l][None, :], g[1][None, :], w_out[l].astype(BF16))
        h2, hp, idx_t, gate_t, cnt = _router(x1, mod3, g[2][None, :], w_router[l], router_bias[l][:, None])
        sched, tok_rows, pos = _dispatch(idx_t, cnt[:, 0].astype(jnp.int32))
        y_rows = _experts(sched, tok_rows, hp, w_exp_gate[l], w_exp_up[l], w_exp_down[l])
        x2 = _final(x1, h2, gate_t.T, y_rows, pos, mod3, g[3][None, :], w_sh_gate[l].astype(BF16), w_sh_up[l].astype(BF16),
                    w_sh_down[l].astype(BF16))
    return x2.reshape(b, s, d)
```
